```python
import jax, jax.numpy as jnp
from jax import lax
import numpy as np

D_MODEL = 2048
BATCH = 8
SEQ = 2048
DEPTH = 2

CHUNK = 64
POOL_WINDOWS = (2, 4, 8, 16)
N_POOL_GROUPS = len(POOL_WINDOWS)
POOL_WIDTH = D_MODEL // 2
POOL_GROUP = POOL_WIDTH // N_POOL_GROUPS
CONV_WIDTH = D_MODEL - POOL_WIDTH
CONV_KERNEL = 31
IN_COLS = POOL_WIDTH + 2 * CONV_WIDTH
N_EXPERTS = 32
TOP_K = 4
D_EXPERT = D_MODEL
SWIGLU_LIMIT = 7.0
SWIGLU_ALPHA = 1.702
LN_EPS = 1e-5
N_MOD = 6

kernel_name = "hybrid_pool_conformer_moe_deepnorm_adaln"


def _layernorm(x, gain=None, bias=None):
    xf = x.astype(jnp.float32)
    mu = jnp.mean(xf, axis=-1, keepdims=True)
    var = jnp.mean(jnp.square(xf - mu), axis=-1, keepdims=True)
    y = (xf - mu) * lax.rsqrt(var + LN_EPS)
    if gain is not None:
        y = y * gain + bias
    return y.astype(x.dtype)


def _modulate(x, shift, scale):
    return _layernorm(x) * (1.0 + scale[:, None, :]) + shift[:, None, :]


def _multiscale_pool(za):
    b, s, _ = za.shape
    zf = za.astype(jnp.float32).reshape(b, s, N_POOL_GROUPS, POOL_GROUP)
    cs = jnp.cumsum(zf, axis=1)
    pos = jnp.arange(1, s + 1, dtype=jnp.float32)[:, None]
    means = []
    for g, w in enumerate(POOL_WINDOWS):
        cg = cs[:, :, g, :]
        lagged = jnp.pad(cg, ((0, 0), (w, 0), (0, 0)))[:, :s]
        means.append((cg - lagged) / jnp.minimum(pos, float(w)))
    pooled = jnp.stack(means, axis=2)
    return (pooled - zf).astype(za.dtype)


def _conformer_conv(zb, w_dw, b_dw, ln_g, ln_b):
    val, gate = jnp.split(zb, 2, axis=-1)
    u = val * jax.nn.sigmoid(gate)
    u = lax.conv_general_dilated(
        u, w_dw[:, None, :], window_strides=(1,),
        padding=[(CONV_KERNEL - 1, 0)],
        dimension_numbers=("NWC", "WIO", "NWC"),
        feature_group_count=CONV_WIDTH) + b_dw
    u = _layernorm(u, ln_g, ln_b)
    return jax.nn.silu(u)


def _moe(h, w_router, b_router, w_gate, b_gate, w_up, b_up, w_down, b_down):
    b, s, d = h.shape
    t = h.reshape(b * s, d)
    logits = (t @ w_router + b_router).astype(jnp.float32)
    top_vals, top_idx = lax.top_k(logits, TOP_K)
    top_w = jax.nn.softmax(top_vals, axis=-1)
    combine = jnp.einsum("tk,tke->te", top_w,
                         jax.nn.one_hot(top_idx, N_EXPERTS, dtype=jnp.float32)).astype(h.dtype)
    out = jnp.zeros_like(t)
    for e in range(N_EXPERTS):
        g = jnp.minimum(t @ w_gate[e] + b_gate[e], SWIGLU_LIMIT)
        u = jnp.clip(t @ w_up[e] + b_up[e], -SWIGLU_LIMIT, SWIGLU_LIMIT)
        act = g * jax.nn.sigmoid(SWIGLU_ALPHA * g) * (u + 1.0)
        out = out + combine[:, e:e + 1] * (act @ w_down[e] + b_down[e])
    return out.reshape(b, s, d)


def setup_inputs(seed: int = 0) -> dict:
    key = jax.random.key(seed)
    ks = jax.random.split(key, 28)
    beta = (8.0 * DEPTH) ** -0.25

    def nrm(k, shape, scale):
        return jax.random.normal(k, shape, dtype=jnp.float32) * scale

    def gain(k, shape):
        return 1.0 + nrm(k, shape, 0.1)

    L, D, E, F = DEPTH, D_MODEL, N_EXPERTS, D_EXPERT
    return {
        "x": nrm(ks[0], (BATCH, SEQ, D), 1.0),
        "c": nrm(ks[1], (BATCH, D), 1.0),
        "w_ada": nrm(ks[2], (L, D, N_MOD * D), 0.5 * D ** -0.5),
        "b_ada": nrm(ks[3], (L, N_MOD * D), 0.02),
        "w_in": nrm(ks[4], (L, D, IN_COLS), D ** -0.5),
        "b_in": nrm(ks[5], (L, IN_COLS), 0.02),
        "w_pool": nrm(ks[6], (L, N_POOL_GROUPS, POOL_GROUP, POOL_GROUP), POOL_GROUP ** -0.5),
        "b_pool": nrm(ks[7], (L, N_POOL_GROUPS, POOL_GROUP), 0.02),
        "pool_scale": gain(ks[8], (L, POOL_WIDTH)),
        "w_dw": nrm(ks[9], (L, CONV_KERNEL, CONV_WIDTH), CONV_KERNEL ** -0.5),
        "b_dw": nrm(ks[10], (L, CONV_WIDTH), 0.02),
        "conv_ln_g": gain(ks[11], (L, CONV_WIDTH)),
        "conv_ln_b": nrm(ks[12], (L, CONV_WIDTH), 0.02),
        "w_out": nrm(ks[13], (L, D, D), beta * D ** -0.5),
        "b_out": nrm(ks[14], (L, D), 0.02),
        "ln1_g": gain(ks[15], (L, D)),
        "ln1_b": nrm(ks[16], (L, D), 0.02),
        "w_router": nrm(ks[17], (L, D, E), D ** -0.5),
        "b_router": nrm(ks[18], (L, E), 0.01),
        "w_gate": nrm(ks[19], (L, E, D, F), D ** -0.5),
        "b_gate": nrm(ks[20], (L, E, F), 0.02),
        "w_up": nrm(ks[21], (L, E, D, F), D ** -0.5),
        "b_up": nrm(ks[22], (L, E, F), 0.02),
        "w_down": nrm(ks[23], (L, E, F, D), beta * F ** -0.5),
        "b_down": nrm(ks[24], (L, E, D), 0.02),
        "ln2_g": gain(ks[25], (L, D)),
        "ln2_b": nrm(ks[26], (L, D), 0.02),
    }


def reference(x, c, w_ada, b_ada, w_in, b_in, w_pool, b_pool, pool_scale, w_dw, b_dw,
              conv_ln_g, conv_ln_b, w_out, b_out, ln1_g, ln1_b, w_router, b_router,
              w_gate, b_gate, w_up, b_up, w_down, b_down, ln2_g, ln2_b):
    alpha = (2.0 * DEPTH) ** 0.25
    b, s, _ = x.shape
    c_act = jax.nn.silu(c)
    for l in range(DEPTH):
        mod = c_act @ w_ada[l] + b_ada[l]
        sh_m, sc_m, g_m, sh_f, sc_f, g_f = jnp.split(mod, N_MOD, axis=-1)

        h = _modulate(x, sh_m, sc_m)
        z = h @ w_in[l] + b_in[l]
        za, zb = z[..., :POOL_WIDTH], z[..., POOL_WIDTH:]
        ya = jnp.einsum("bsgc,gcd->bsgd", _multiscale_pool(za), w_pool[l]) + b_pool[l]
        ya = ya.reshape(b, s, POOL_WIDTH) * pool_scale[l]
        yb = _conformer_conv(zb, w_dw[l], b_dw[l], conv_ln_g[l], conv_ln_b[l])
        y = jnp.concatenate([ya, yb], axis=-1) @ w_out[l] + b_out[l]
        x = _layernorm(alpha * x + g_m[:, None, :] * y, ln1_g[l], ln1_b[l])

        h = _modulate(x, sh_f, sc_f)
        f = _moe(h, w_router[l], b_router[l], w_gate[l], b_gate[l], w_up[l], b_up[l],
                 w_down[l], b_down[l])
        x = _layernorm(alpha * x + g_f[:, None, :] * f, ln2_g[l], ln2_b[l])
    return x
```

```python
import functools
from typing import NamedTuple

import jax
import jax.numpy as jnp
from jax import lax
from jax.experimental import pallas as pl
from jax.experimental.pallas import tpu as pltpu

POOL_WINDOWS = (2, 4, 8, 16)
CONV_KERNEL = 31
TOP_K = 4
SWIGLU_LIMIT = 7.0
SWIGLU_ALPHA = 1.702
LN_EPS = 1e-5
N_MOD = 6

HALO = 32
LANES = 128
VMEM_LIMIT = 56 * 1024 * 1024

f32 = jnp.float32
bf16 = jnp.bfloat16
i32 = jnp.int32


class Plan(NamedTuple):
    tn_ada: int
    tm_in: int
    tm_mix: int
    conv_rows: int
    tb_route: int
    tm_moe: int
    tf_moe: int
    tc_disp: int
    tc_comb: int


def _plan(seq, d_model, n_tok, d_expert):
    def fit(pref, n):
        t = min(pref, n)
        assert n % t == 0, (pref, n)
        return t
    return Plan(
        tn_ada=fit(1024, N_MOD * d_model),
        tm_in=fit(512, seq),
        tm_mix=fit(256, seq),
        conv_rows=64,
        tb_route=fit(512, n_tok),
        tm_moe=fit(512, n_tok * TOP_K),
        tf_moe=fit(512, d_expert),
        tc_disp=fit(1024, n_tok),
        tc_comb=fit(128, n_tok),
    )


def _layernorm(x):
    mu = jnp.mean(x, axis=-1, keepdims=True)
    xc = x - mu
    var = jnp.mean(xc * xc, axis=-1, keepdims=True)
    return xc * lax.rsqrt(var + LN_EPS)


def _silu(x):
    return x * jax.nn.sigmoid(x)


def _params(*sem):
    return pltpu.CompilerParams(dimension_semantics=sem, vmem_limit_bytes=VMEM_LIMIT)


def _ada_kernel(c_ref, w_ref, b_ref, o_ref):
    c = c_ref[...]
    o_ref[...] = jnp.dot(_silu(c).astype(bf16), w_ref[...].astype(bf16),
                         preferred_element_type=f32) + b_ref[...]


def _ada(c, w_ada, b_ada, plan):
    n_layer, d, n = w_ada.shape
    b = c.shape[0]
    tn = plan.tn_ada
    return pl.pallas_call(
        _ada_kernel,
        out_shape=jax.ShapeDtypeStruct((n_layer, b, n), f32),
        grid=(n_layer, n // tn),
        in_specs=[
            pl.BlockSpec((b, d), lambda l, j: (0, 0)),
            pl.BlockSpec((None, d, tn), lambda l, j: (l, 0, j)),
            pl.BlockSpec((None, 1, tn), lambda l, j: (l, 0, j)),
        ],
        out_specs=pl.BlockSpec((None, b, tn), lambda l, j: (l, 0, j)),
        compiler_params=_params("arbitrary", "arbitrary"),
        name="ada",
    )(c, w_ada, b_ada.reshape(n_layer, 1, n))


def _inproj_kernel(x_ref, sh_ref, sc_ref, w_ref, b_ref, z_ref):
    h = _layernorm(x_ref[...]) * (1.0 + sc_ref[...]) + sh_ref[...]
    z_ref[...] = jnp.dot(h.astype(bf16), w_ref[...], preferred_element_type=f32) + b_ref[...]


def _inproj(x, sh, sc, w_in, b_in, seq, plan):
    n_tok, d = x.shape
    n = w_in.shape[1]
    tm = plan.tm_in
    per_seq = seq // tm
    vec = pl.BlockSpec((None, 1, d), lambda i: (i // per_seq, 0, 0))
    return pl.pallas_call(
        _inproj_kernel,
        out_shape=jax.ShapeDtypeStruct((n_tok, n), f32),
        grid=(n_tok // tm,),
        in_specs=[
            pl.BlockSpec((tm, d), lambda i: (i, 0)),
            vec, vec,
            pl.BlockSpec((d, n), lambda i: (0, 0)),
            pl.BlockSpec((1, n), lambda i: (0, 0)),
        ],
        out_specs=pl.BlockSpec((tm, n), lambda i: (i, 0)),
        compiler_params=_params("arbitrary"),
        name="inproj",
    )(x, sh, sc, w_in, b_in.reshape(1, n))


def _mix_kernel(zc_ref, zp_ref, x_ref, gm_ref, shf_ref, scf_ref,
                wpool_ref, bpool_ref, pscale_ref, wdw_ref, bdw_ref, cg_ref, cb_ref,
                wout_ref, bout_ref, l1g_ref, l1b_ref, wr_ref, br_ref,
                x1_ref, h2_ref, lg_ref,
                zbuf, ubuf, cbuf, ycat, *, per_seq, alpha, conv_rows):
    tm, d = x_ref.shape
    pw = d // 2
    cw = d - pw
    grp = pw // len(POOL_WINDOWS)
    i = pl.program_id(0)
    tile_in_seq = i % per_seq
    first = tile_in_seq == 0

    zbuf[0:HALO, :] = jnp.where(first, 0.0, zp_ref[:, 0:pw])
    zbuf[HALO:, :] = zc_ref[:, 0:pw]
    ubuf[0:HALO, :] = jnp.where(first, 0.0, zp_ref[:, pw:pw + cw] * jax.nn.sigmoid(zp_ref[:, pw + cw:]))
    ubuf[HALO:, :] = zc_ref[:, pw:pw + cw] * jax.nn.sigmoid(zc_ref[:, pw + cw:])

    t_pos = (tile_in_seq * tm + lax.broadcasted_iota(i32, (tm, 1), 0) + 1).astype(f32)
    for g, w in enumerate(POOL_WINDOWS):
        cols = slice(g * grp, (g + 1) * grp)
        tok = zbuf[HALO:HALO + tm, cols]
        win = tok
        for k in range(1, w):
            win = win + zbuf[HALO - k:HALO - k + tm, cols]
        pooled = win / jnp.minimum(t_pos, float(w)) - tok
        ya = jnp.dot(pooled.astype(bf16), wpool_ref[g], preferred_element_type=f32) + bpool_ref[:, cols]
        ycat[:, cols] = (ya * pscale_ref[:, cols]).astype(bf16)

    def conv_lanes(c, carry):
        lanes = pl.ds(pl.multiple_of(c * LANES, LANES), LANES)
        for r0 in range(0, tm, conv_rows):
            acc = jnp.zeros((conv_rows, LANES), f32)
            for k in range(CONV_KERNEL):
                row = HALO - (CONV_KERNEL - 1) + k + r0
                acc = acc + wdw_ref[k:k + 1, lanes] * ubuf[row:row + conv_rows, lanes]
            cbuf[r0:r0 + conv_rows, lanes] = acc + bdw_ref[:, lanes]
        return carry
    lax.fori_loop(0, cw // LANES, conv_lanes, 0)
    yb = _silu(_layernorm(cbuf[...]) * cg_ref[...] + cb_ref[...])
    ycat[:, pw:] = yb.astype(bf16)

    y = jnp.dot(ycat[...], wout_ref[...], preferred_element_type=f32) + bout_ref[...]
    x1 = _layernorm(alpha * x_ref[...] + gm_ref[...] * y) * l1g_ref[...] + l1b_ref[...]
    x1_ref[...] = x1
    h2 = _layernorm(x1) * (1.0 + scf_ref[...]) + shf_ref[...]
    h2_ref[...] = h2
    lg_ref[...] = jnp.dot(h2.astype(bf16), wr_ref[...], preferred_element_type=f32) + br_ref[...]


def _mix(z, x, gm, shf, scf, lw, seq, alpha, plan):
    n_tok, d = x.shape
    nz = z.shape[1]
    n_exp = lw["w_router"].shape[1]
    pw = d // 2
    cw = d - pw
    tm = plan.tm_mix
    per_seq = seq // tm
    halo_per_tile = tm // HALO
    vec = pl.BlockSpec((None, 1, d), lambda i: (i // per_seq, 0, 0))

    def whole(a):
        return pl.BlockSpec(a.shape, lambda i: (0,) * a.ndim)

    weights = [lw["w_pool"], lw["b_pool"], lw["pool_scale"], lw["w_dw"], lw["b_dw"], lw["conv_ln_g"],
               lw["conv_ln_b"], lw["w_out"], lw["b_out"], lw["ln1_g"], lw["ln1_b"], lw["w_router"], lw["b_router"]]
    kern = functools.partial(_mix_kernel, per_seq=per_seq, alpha=alpha, conv_rows=min(plan.conv_rows, tm))
    return pl.pallas_call(
        kern,
        out_shape=(jax.ShapeDtypeStruct((n_tok, d), f32),
                   jax.ShapeDtypeStruct((n_tok, d), f32),
                   jax.ShapeDtypeStruct((n_tok, n_exp), f32)),
        grid=(n_tok // tm,),
        in_specs=[
            pl.BlockSpec((tm, nz), lambda i: (i, 0)),
            pl.BlockSpec((HALO, nz), lambda i: (jnp.maximum(i * halo_per_tile - 1, 0), 0)),
            pl.BlockSpec((tm, d), lambda i: (i, 0)),
            vec, vec, vec,
        ] + [whole(a) for a in weights],
        out_specs=(pl.BlockSpec((tm, d), lambda i: (i, 0)),
                   pl.BlockSpec((tm, d), lambda i: (i, 0)),
                   pl.BlockSpec((tm, n_exp), lambda i: (i, 0))),
        scratch_shapes=[pltpu.VMEM((tm + HALO, pw), f32), pltpu.VMEM((tm + HALO, cw), f32),
                        pltpu.VMEM((tm, cw), f32), pltpu.VMEM((tm, d), bf16)],
        compiler_params=_params("arbitrary"),
        name="mix",
    )(z, z, x, gm, shf, scf, *weights)


def _route_kernel(lg_ref, w_ref, pos_ref, cnt_ref, idx_s, rank_s, *, tb, granule):
    n_exp, n_tok = lg_ref.shape
    iota_e = lax.broadcasted_iota(i32, (n_exp, tb), 0)
    before = (lax.broadcasted_iota(i32, (tb, tb), 0) < lax.broadcasted_iota(i32, (tb, tb), 1)).astype(bf16)

    def select(b, seen):
        blk = pl.ds(pl.multiple_of(b * tb, tb), tb)
        work = lg_ref[:, blk]
        vals, sels = [], []
        for k in range(TOP_K):
            m = jnp.max(work, axis=0, keepdims=True)
            idx = jnp.min(jnp.where(work == m, iota_e, n_exp), axis=0, keepdims=True)
            sel = iota_e == idx
            vals.append(m)
            sels.append(sel)
            idx_s[k:k + 1, blk] = idx
            work = jnp.where(sel, -jnp.inf, work)
        exps = [jnp.exp(v - vals[0]) for v in vals]
        den = exps[0]
        for e in exps[1:]:
            den = den + e
        chosen = jnp.zeros((n_exp, tb), f32)
        for k in range(TOP_K):
            w_ref[k:k + 1, blk] = exps[k] / den
            chosen = chosen + sels[k].astype(f32)
        earlier = jnp.dot(chosen.astype(bf16), before, preferred_element_type=f32) + seen
        for k in range(TOP_K):
            rank_s[k:k + 1, blk] = jnp.sum(jnp.where(sels[k], earlier, 0.0), axis=0, keepdims=True)
        return seen + jnp.sum(chosen, axis=1, keepdims=True)

    counts = lax.fori_loop(0, n_tok // tb, select, jnp.zeros((n_exp, 1), f32))

    padded = jnp.ceil(counts / granule) * granule
    ee_r = lax.broadcasted_iota(i32, (n_exp, n_exp), 0)
    ee_c = lax.broadcasted_iota(i32, (n_exp, n_exp), 1)
    padded_row = jnp.sum(jnp.where(ee_r == ee_c, padded, 0.0), axis=0, keepdims=True)
    starts = jnp.sum(jnp.where(ee_c < ee_r, padded_row, 0.0), axis=1, keepdims=True)

    def place(b, carry):
        blk = pl.ds(pl.multiple_of(b * tb, tb), tb)
        for k in range(TOP_K):
            sel = iota_e == idx_s[k:k + 1, blk]
            start = jnp.sum(jnp.where(sel, starts, 0.0), axis=0, keepdims=True)
            pos_ref[k:k + 1, blk] = (start + rank_s[k:k + 1, blk]).astype(i32)
        return carry

    lax.fori_loop(0, n_tok // tb, place, 0)
    cnt_ref[...] = jnp.broadcast_to(counts, cnt_ref.shape).astype(i32)


def _route(logits_t, plan):
    n_exp, n_tok = logits_t.shape
    kern = functools.partial(_route_kernel, tb=plan.tb_route, granule=float(plan.tm_moe))
    return pl.pallas_call(
        kern,
        out_shape=(jax.ShapeDtypeStruct((TOP_K, n_tok), f32),
                   jax.ShapeDtypeStruct((TOP_K, n_tok), i32),
                   jax.ShapeDtypeStruct((n_exp, LANES), i32)),
        scratch_shapes=[pltpu.VMEM((TOP_K, n_tok), i32), pltpu.VMEM((TOP_K, n_tok), f32)],
        compiler_params=pltpu.CompilerParams(vmem_limit_bytes=VMEM_LIMIT),
        name="route",
    )(logits_t)


def _dispatch_kernel(pstart_ref, pcnt_ref, nv_ref, pos_ref, h_ref, xs_ref, zrows, sem, zsem, *, tc, tm):
    c = pl.program_id(0)
    n_exp = pstart_ref.shape[0]
    zr = zrows.shape[0]

    def row_copy(t, p):
        return pltpu.make_async_copy(h_ref.at[pl.ds(t, 1)], xs_ref.at[pl.ds(p, 1)], sem)

    def issue(j, carry):
        for k in range(TOP_K):
            row_copy(c * tc + j, pos_ref[0, k * tc + j]).start()
        return carry
    lax.fori_loop(0, tc, issue, 0)

    step_rows = xs_ref.at[pl.ds(0, TOP_K * tc)]
    pltpu.make_async_copy(step_rows, step_rows, sem).wait()

    @pl.when(c == pl.num_programs(0) - 1)
    def _():
        zrows[...] = jnp.zeros_like(zrows)

        def pad_copy(p):
            return pltpu.make_async_copy(zrows.at[pl.ds(0, 1)], xs_ref.at[pl.ds(p, 1)], zsem)

        def per_expert(e, carry):
            s = pstart_ref[e]
            n = pcnt_ref[e]
            lax.fori_loop(0, n, lambda j, a: (pad_copy(s + j).start(), a)[1], 0)
            lax.fori_loop(0, n, lambda j, a: (pad_copy(0).wait(), a)[1], 0)
            return carry
        lax.fori_loop(0, n_exp, per_expert, 0)

        def tail_copy(r):
            return pltpu.make_async_copy(zrows, xs_ref.at[pl.ds(pl.multiple_of(r * zr, zr), zr)], zsem)

        first = nv_ref[0] * (tm // zr)
        last = xs_ref.shape[0] // zr
        lax.fori_loop(first, last, lambda r, a: (tail_copy(r).start(), a)[1], 0)
        lax.fori_loop(first, last, lambda r, a: (tail_copy(0).wait(), a)[1], 0)


def _dispatch(h2, pos_blocks, pad_start, pad_cnt, n_valid, n_rows, plan):
    n_tok, d = h2.shape
    tc = plan.tc_disp
    zr = min(plan.tm_moe, 256)
    assert plan.tm_moe % zr == 0
    kern = functools.partial(_dispatch_kernel, tc=tc, tm=plan.tm_moe)
    return pl.pallas_call(
        kern,
        out_shape=jax.ShapeDtypeStruct((n_rows, d), f32),
        grid_spec=pltpu.PrefetchScalarGridSpec(
            num_scalar_prefetch=3,
            grid=(n_tok // tc,),
            in_specs=[
                pl.BlockSpec((None, 1, TOP_K * tc), lambda c, *_: (c, 0, 0), memory_space=pltpu.SMEM),
                pl.BlockSpec(memory_space=pl.ANY),
            ],
            out_specs=pl.BlockSpec(memory_space=pl.ANY),
            scratch_shapes=[pltpu.VMEM((zr, d), f32), pltpu.SemaphoreType.DMA, pltpu.SemaphoreType.DMA],
        ),
        compiler_params=_params("arbitrary"),
        name="dispatch",
    )(pad_start, pad_cnt, n_valid, pos_blocks, h2)


def _moe_kernel(te_ref, nv_ref, x_ref, wg_ref, bg_ref, wu_ref, bu_ref, wd_ref, bd_ref, o_ref, xb, act,
                *, n_chunk, tf):
    i = pl.program_id(0)
    j = pl.program_id(1)
    valid = i < nv_ref[0]

    @pl.when(jnp.logical_and(valid, j == 0))
    def _():
        xb[...] = x_ref[...].astype(bf16)

    @pl.when(jnp.logical_and(valid, j < n_chunk))
    def _():
        x = xb[...]
        g = jnp.dot(x, wg_ref[...].astype(bf16), preferred_element_type=f32) + bg_ref[...]
        u = jnp.dot(x, wu_ref[...].astype(bf16), preferred_element_type=f32) + bu_ref[...]
        g = jnp.minimum(g, SWIGLU_LIMIT)
        u = jnp.clip(u, -SWIGLU_LIMIT, SWIGLU_LIMIT)
        a = g * jax.nn.sigmoid(SWIGLU_ALPHA * g) * (u + 1.0)
        act[:, pl.ds(pl.multiple_of(j * tf, tf), tf)] = a.astype(bf16)

    @pl.when(jnp.logical_and(valid, j >= n_chunk))
    def _():
        o_ref[...] = jnp.dot(act[...], wd_ref[...].astype(bf16), preferred_element_type=f32) + bd_ref[...]

    @pl.when(jnp.logical_and(jnp.logical_not(valid), j >= n_chunk))
    def _():
        o_ref[...] = jnp.zeros_like(o_ref)


def _moe(xs, tile_expert, n_valid, w_gate, b_gate, w_up, b_up, w_down, b_down, layer, plan):
    n_rows, d = xs.shape
    n_layer, n_exp, _, f = w_gate.shape
    tm, tf = plan.tm_moe, plan.tf_moe
    td = min(tf, d)
    assert f % tf == 0 and d % td == 0 and f // tf == d // td
    n_chunk = f // tf
    n_tile = n_rows // tm

    def tile(i, nv):
        return jnp.minimum(i, nv[0] - 1)

    def phase(i, j, nv):
        return jnp.where(i < nv[0], j, 2 * n_chunk - 1)

    def up_idx(i, j, te, nv):
        return (layer, te[tile(i, nv)], 0, jnp.minimum(phase(i, j, nv), n_chunk - 1))

    def down_idx(i, j, te, nv):
        return (layer, te[tile(i, nv)], 0, jnp.maximum(phase(i, j, nv) - n_chunk, 0))

    kern = functools.partial(_moe_kernel, n_chunk=n_chunk, tf=tf)
    return pl.pallas_call(
        kern,
        out_shape=jax.ShapeDtypeStruct((n_rows, d), f32),
        grid_spec=pltpu.PrefetchScalarGridSpec(
            num_scalar_prefetch=2,
            grid=(n_tile, 2 * n_chunk),
            in_specs=[
                pl.BlockSpec((tm, d), lambda i, j, te, nv: (tile(i, nv), 0)),
                pl.BlockSpec((None, None, d, tf), up_idx),
                pl.BlockSpec((None, None, 1, tf), up_idx),
                pl.BlockSpec((None, None, d, tf), up_idx),
                pl.BlockSpec((None, None, 1, tf), up_idx),
                pl.BlockSpec((None, None, f, td), down_idx),
                pl.BlockSpec((None, None, 1, td), down_idx),
            ],
            out_specs=pl.BlockSpec((tm, td), lambda i, j, te, nv: (i, jnp.maximum(j - n_chunk, 0))),
            scratch_shapes=[pltpu.VMEM((tm, d), bf16), pltpu.VMEM((tm, f), bf16)],
        ),
        compiler_params=_params("arbitrary", "arbitrary"),
        name="moe",
    )(tile_expert, n_valid, xs, w_gate, b_gate.reshape(n_layer, n_exp, 1, f), w_up,
      b_up.reshape(n_layer, n_exp, 1, f), w_down, b_down.reshape(n_layer, n_exp, 1, d))


def _combine_kernel(pos_ref, nxt_ref, ys_ref, w_ref, x1_ref, gf_ref, g_ref, b_ref, o_ref, buf, sem,
                    *, tc, alpha):
    i = pl.program_id(0)
    n = pl.num_programs(0)
    slot = i % 2
    rows = TOP_K * tc

    def gather(p_ref, s):
        def issue(j, carry):
            for k in range(TOP_K):
                pltpu.make_async_copy(ys_ref.at[pl.ds(p_ref[0, k * tc + j], 1)],
                                      buf.at[s, pl.ds(k * tc + j, 1)], sem.at[s]).start()
            return carry
        lax.fori_loop(0, tc, issue, 0)

    @pl.when(i == 0)
    def _():
        gather(pos_ref, 0)

    @pl.when(i + 1 < n)
    def _():
        gather(nxt_ref, 1 - slot)

    pltpu.make_async_copy(ys_ref.at[pl.ds(0, rows)], buf.at[slot], sem.at[slot]).wait()

    f = w_ref[:, 0:1] * buf[slot, 0:tc, :]
    for k in range(1, TOP_K):
        f = f + w_ref[:, k:k + 1] * buf[slot, k * tc:(k + 1) * tc, :]
    o_ref[...] = _layernorm(alpha * x1_ref[...] + gf_ref[...] * f) * g_ref[...] + b_ref[...]


def _combine(ys, pos_blocks, top_w, x1, gf, ln_g, ln_b, seq, alpha, plan):
    n_tok, d = x1.shape
    tc = plan.tc_comb
    n_step = n_tok // tc
    per_seq = seq // tc
    kern = functools.partial(_combine_kernel, tc=tc, alpha=alpha)
    smem = functools.partial(pl.BlockSpec, (None, 1, TOP_K * tc), memory_space=pltpu.SMEM)
    return pl.pallas_call(
        kern,
        out_shape=jax.ShapeDtypeStruct((n_tok, d), f32),
        grid=(n_step,),
        in_specs=[
            smem(lambda i: (i, 0, 0)),
            smem(lambda i: (jnp.minimum(i + 1, n_step - 1), 0, 0)),
            pl.BlockSpec(memory_space=pl.ANY),
            pl.BlockSpec((tc, TOP_K), lambda i: (i, 0)),
            pl.BlockSpec((tc, d), lambda i: (i, 0)),
            pl.BlockSpec((None, 1, d), lambda i: (i // per_seq, 0, 0)),
            pl.BlockSpec((1, d), lambda i: (0, 0)),
            pl.BlockSpec((1, d), lambda i: (0, 0)),
        ],
        out_specs=pl.BlockSpec((tc, d), lambda i: (i, 0)),
        scratch_shapes=[pltpu.VMEM((2, TOP_K * tc, d), f32), pltpu.SemaphoreType.DMA((2,))],
        compiler_params=_params("arbitrary"),
        name="combine",
    )(pos_blocks, pos_blocks, ys, top_w, x1, gf, ln_g.reshape(1, d), ln_b.reshape(1, d))


def _pos_blocks(pos, tc):
    k, n_tok = pos.shape
    return pos.reshape(k, n_tok // tc, tc).transpose(1, 0, 2).reshape(n_tok // tc, 1, k * tc)


def _tile_table(counts, tm, n_tile):
    tiles = (counts + tm - 1) // tm
    ends = jnp.cumsum(tiles)
    n_valid = ends[-1:]
    tile_expert = jnp.searchsorted(ends, jnp.arange(n_tile, dtype=i32), side="right").astype(i32)
    tile_expert = jnp.minimum(tile_expert, counts.shape[0] - 1)
    pad_start = (ends - tiles) * tm + counts
    pad_cnt = tiles * tm - counts
    return tile_expert, n_valid.astype(i32), pad_start.astype(i32), pad_cnt.astype(i32)


def kernel(x, c, w_ada, b_ada, w_in, b_in, w_pool, b_pool, pool_scale, w_dw, b_dw, conv_ln_g, conv_ln_b,
           w_out, b_out, ln1_g, ln1_b, w_router, b_router, w_gate, b_gate, w_up, b_up, w_down, b_down,
           ln2_g, ln2_b):
    batch, seq, d = x.shape
    depth = w_ada.shape[0]
    n_exp, f = w_gate.shape[1], w_gate.shape[3]
    n_tok = batch * seq
    alpha = (2.0 * depth) ** 0.25
    plan = _plan(seq, d, n_tok, f)
    n_tile = n_tok * TOP_K // plan.tm_moe + n_exp
    n_rows = n_tile * plan.tm_moe

    mod = _ada(c, w_ada, b_ada, plan)
    xt = x.reshape(n_tok, d)
    for l in range(depth):
        sh_m, sc_m, g_m, sh_f, sc_f, g_f = [m.reshape(batch, 1, d) for m in jnp.split(mod[l], N_MOD, axis=-1)]
        row = lambda a: a[l].reshape(1, -1)
        lw = dict(
            w_pool=w_pool[l].astype(bf16), b_pool=row(b_pool), pool_scale=row(pool_scale),
            w_dw=w_dw[l], b_dw=row(b_dw), conv_ln_g=row(conv_ln_g), conv_ln_b=row(conv_ln_b),
            w_out=w_out[l].astype(bf16), b_out=row(b_out), ln1_g=row(ln1_g), ln1_b=row(ln1_b),
            w_router=w_router[l].astype(bf16), b_router=row(b_router))
        z = _inproj(xt, sh_m, sc_m, w_in[l].astype(bf16), b_in[l], seq, plan)
        x1, h2, logits = _mix(z, xt, g_m, sh_f, sc_f, lw, seq, alpha, plan)
        top_w, pos, counts = _route(logits.T, plan)
        tile_expert, n_valid, pad_start, pad_cnt = _tile_table(counts[:, 0], plan.tm_moe, n_tile)
        xs = _dispatch(h2, _pos_blocks(pos, plan.tc_disp), pad_start, pad_cnt, n_valid, n_rows, plan)
        ys = _moe(xs, tile_expert, n_valid, w_gate, b_gate, w_up, b_up, w_down, b_down, l, plan)
        xt = _combine(ys, _pos_blocks(pos, plan.tc_comb), top_w.T, x1, g_f, ln2_g[l], ln2_b[l], seq, alpha, plan)
    return xt.reshape(batch, seq, d)
```

```python
import functools
from typing import NamedTuple

import jax
import jax.numpy as jnp
from jax import lax
from jax.experimental import pallas as pl
from jax.experimental.pallas import tpu as pltpu

POOL_WINDOWS = (2, 4, 8, 16)
CONV_KERNEL = 31
TOP_K = 4
SWIGLU_LIMIT = 7.0
SWIGLU_ALPHA = 1.702
LN_EPS = 1e-5
N_MOD = 6

HALO = 32
LANES = 128
VMEM_LIMIT = 56 * 1024 * 1024

f32 = jnp.float32
bf16 = jnp.bfloat16
i32 = jnp.int32
u32 = jnp.uint32


class Plan(NamedTuple):
    tn_ada: int
    tm_in: int
    tm_mix: int
    conv_rows: int
    tb_route: int
    sub_moe: int
    nsub_moe: int
    tf_moe: int
    td_moe: int
    tc_disp: int
    tc_comb: int

    @property
    def tm_moe(self):
        return self.sub_moe * self.nsub_moe


def _plan(seq, d_model, n_tok, d_expert):
    def fit(pref, n):
        t = min(pref, n)
        assert n % t == 0, (pref, n)
        return t
    return Plan(
        tn_ada=fit(1024, N_MOD * d_model),
        tm_in=fit(512, seq),
        tm_mix=fit(256, seq),
        conv_rows=64,
        tb_route=fit(512, n_tok),
        sub_moe=256,
        nsub_moe=4,
        tf_moe=fit(512, d_expert),
        td_moe=fit(512, d_model),
        tc_disp=fit(1024, n_tok),
        tc_comb=fit(128, n_tok),
    )


def _layernorm(x):
    mu = jnp.mean(x, axis=-1, keepdims=True)
    xc = x - mu
    var = jnp.mean(xc * xc, axis=-1, keepdims=True)
    return xc * lax.rsqrt(var + LN_EPS)


def _silu(x):
    return x * jax.nn.sigmoid(x)


def _pack_halves(xb):
    half = xb.shape[1] // 2
    lo = lax.bitcast_convert_type(xb[:, :half].astype(f32), u32) >> 16
    hi = lax.bitcast_convert_type(xb[:, half:].astype(f32), u32) & jnp.uint32(0xFFFF0000)
    return lo | hi


def _unpack_halves(w):
    lo = lax.bitcast_convert_type(w << 16, f32).astype(bf16)
    hi = lax.bitcast_convert_type(w & jnp.uint32(0xFFFF0000), f32).astype(bf16)
    return lo, hi


def _params(*sem):
    return pltpu.CompilerParams(dimension_semantics=sem, vmem_limit_bytes=VMEM_LIMIT)


def _ada_kernel(c_ref, w_ref, b_ref, o_ref):
    c = c_ref[...]
    o_ref[...] = jnp.dot(_silu(c).astype(bf16), w_ref[...].astype(bf16),
                         preferred_element_type=f32) + b_ref[...]


def _ada(c, w_ada, b_ada, plan):
    n_layer, d, n = w_ada.shape
    b = c.shape[0]
    tn = plan.tn_ada
    return pl.pallas_call(
        _ada_kernel,
        out_shape=jax.ShapeDtypeStruct((n_layer, b, n), f32),
        grid=(n_layer, n // tn),
        in_specs=[
            pl.BlockSpec((b, d), lambda l, j: (0, 0)),
            pl.BlockSpec((None, d, tn), lambda l, j: (l, 0, j)),
            pl.BlockSpec((None, 1, tn), lambda l, j: (l, 0, j)),
        ],
        out_specs=pl.BlockSpec((None, b, tn), lambda l, j: (l, 0, j)),
        compiler_params=_params("arbitrary", "arbitrary"),
        name="ada",
    )(c, w_ada, b_ada.reshape(n_layer, 1, n))


def _inproj_kernel(x_ref, sh_ref, sc_ref, w_ref, b_ref, z_ref):
    h = _layernorm(x_ref[...]) * (1.0 + sc_ref[...]) + sh_ref[...]
    z_ref[...] = jnp.dot(h.astype(bf16), w_ref[...], preferred_element_type=f32) + b_ref[...]


def _inproj(x, sh, sc, w_in, b_in, seq, plan):
    n_tok, d = x.shape
    n = w_in.shape[1]
    tm = plan.tm_in
    per_seq = seq // tm
    vec = pl.BlockSpec((None, 1, d), lambda i: (i // per_seq, 0, 0))
    return pl.pallas_call(
        _inproj_kernel,
        out_shape=jax.ShapeDtypeStruct((n_tok, n), f32),
        grid=(n_tok // tm,),
        in_specs=[
            pl.BlockSpec((tm, d), lambda i: (i, 0)),
            vec, vec,
            pl.BlockSpec((d, n), lambda i: (0, 0)),
            pl.BlockSpec((1, n), lambda i: (0, 0)),
        ],
        out_specs=pl.BlockSpec((tm, n), lambda i: (i, 0)),
        compiler_params=_params("arbitrary"),
        name="inproj",
    )(x, sh, sc, w_in, b_in.reshape(1, n))


def _mix_kernel(zc_ref, zp_ref, x_ref, gm_ref, shf_ref, scf_ref,
                wpool_ref, bpool_ref, pscale_ref, wdw_ref, bdw_ref, cg_ref, cb_ref,
                wout_ref, bout_ref, l1g_ref, l1b_ref, wr_ref, br_ref,
                x1_ref, h2_ref, lg_ref,
                zbuf, ubuf, cbuf, ycat, *, per_seq, alpha, conv_rows):
    tm, d = x_ref.shape
    pw = d // 2
    cw = d - pw
    grp = pw // len(POOL_WINDOWS)
    i = pl.program_id(0)
    tile_in_seq = i % per_seq
    first = tile_in_seq == 0

    zbuf[0:HALO, :] = jnp.where(first, 0.0, zp_ref[:, 0:pw])
    zbuf[HALO:, :] = zc_ref[:, 0:pw]
    ubuf[0:HALO, :] = jnp.where(first, 0.0, zp_ref[:, pw:pw + cw] * jax.nn.sigmoid(zp_ref[:, pw + cw:]))
    ubuf[HALO:, :] = zc_ref[:, pw:pw + cw] * jax.nn.sigmoid(zc_ref[:, pw + cw:])

    t_pos = (tile_in_seq * tm + lax.broadcasted_iota(i32, (tm, 1), 0) + 1).astype(f32)
    for g, w in enumerate(POOL_WINDOWS):
        cols = slice(g * grp, (g + 1) * grp)
        tok = zbuf[HALO:HALO + tm, cols]
        win = tok
        for k in range(1, w):
            win = win + zbuf[HALO - k:HALO - k + tm, cols]
        pooled = win / jnp.minimum(t_pos, float(w)) - tok
        ya = jnp.dot(pooled.astype(bf16), wpool_ref[g], preferred_element_type=f32) + bpool_ref[:, cols]
        ycat[:, cols] = (ya * pscale_ref[:, cols]).astype(bf16)

    def conv_lanes(c, carry):
        lanes = pl.ds(pl.multiple_of(c * LANES, LANES), LANES)
        for r0 in range(0, tm, conv_rows):
            acc = jnp.zeros((conv_rows, LANES), f32)
            for k in range(CONV_KERNEL):
                row = HALO - (CONV_KERNEL - 1) + k + r0
                acc = acc + wdw_ref[k:k + 1, lanes] * ubuf[row:row + conv_rows, lanes]
            cbuf[r0:r0 + conv_rows, lanes] = acc + bdw_ref[:, lanes]
        return carry
    lax.fori_loop(0, cw // LANES, conv_lanes, 0)
    yb = _silu(_layernorm(cbuf[...]) * cg_ref[...] + cb_ref[...])
    ycat[:, pw:] = yb.astype(bf16)

    y = jnp.dot(ycat[...], wout_ref[...], preferred_element_type=f32) + bout_ref[...]
    x1 = _layernorm(alpha * x_ref[...] + gm_ref[...] * y) * l1g_ref[...] + l1b_ref[...]
    x1_ref[...] = x1
    h2 = (_layernorm(x1) * (1.0 + scf_ref[...]) + shf_ref[...]).astype(bf16)
    h2_ref[...] = _pack_halves(h2)
    lg_ref[...] = jnp.dot(h2, wr_ref[...], preferred_element_type=f32) + br_ref[...]


def _mix(z, x, gm, shf, scf, lw, seq, alpha, plan):
    n_tok, d = x.shape
    nz = z.shape[1]
    n_exp = lw["w_router"].shape[1]
    pw = d // 2
    cw = d - pw
    tm = plan.tm_mix
    per_seq = seq // tm
    halo_per_tile = tm // HALO
    vec = pl.BlockSpec((None, 1, d), lambda i: (i // per_seq, 0, 0))

    def whole(a):
        return pl.BlockSpec(a.shape, lambda i: (0,) * a.ndim)

    weights = [lw["w_pool"], lw["b_pool"], lw["pool_scale"], lw["w_dw"], lw["b_dw"], lw["conv_ln_g"],
               lw["conv_ln_b"], lw["w_out"], lw["b_out"], lw["ln1_g"], lw["ln1_b"], lw["w_router"], lw["b_router"]]
    kern = functools.partial(_mix_kernel, per_seq=per_seq, alpha=alpha, conv_rows=min(plan.conv_rows, tm))
    return pl.pallas_call(
        kern,
        out_shape=(jax.ShapeDtypeStruct((n_tok, d), f32),
                   jax.ShapeDtypeStruct((n_tok, d // 2), u32),
                   jax.ShapeDtypeStruct((n_tok, n_exp), f32)),
        grid=(n_tok // tm,),
        in_specs=[
            pl.BlockSpec((tm, nz), lambda i: (i, 0)),
            pl.BlockSpec((HALO, nz), lambda i: (jnp.maximum(i * halo_per_tile - 1, 0), 0)),
            pl.BlockSpec((tm, d), lambda i: (i, 0)),
            vec, vec, vec,
        ] + [whole(a) for a in weights],
        out_specs=(pl.BlockSpec((tm, d), lambda i: (i, 0)),
                   pl.BlockSpec((tm, d // 2), lambda i: (i, 0)),
                   pl.BlockSpec((tm, n_exp), lambda i: (i, 0))),
        scratch_shapes=[pltpu.VMEM((tm + HALO, pw), f32), pltpu.VMEM((tm + HALO, cw), f32),
                        pltpu.VMEM((tm, cw), f32), pltpu.VMEM((tm, d), bf16)],
        compiler_params=_params("arbitrary"),
        name="mix",
    )(z, z, x, gm, shf, scf, *weights)


def _route_kernel(lg_ref, w_ref, pos_ref, cnt_ref, idx_s, rank_s, *, tb, granule):
    n_exp, n_tok = lg_ref.shape
    iota_e = lax.broadcasted_iota(i32, (n_exp, tb), 0)
    before = (lax.broadcasted_iota(i32, (tb, tb), 0) < lax.broadcasted_iota(i32, (tb, tb), 1)).astype(bf16)

    def select(b, seen):
        blk = pl.ds(pl.multiple_of(b * tb, tb), tb)
        work = lg_ref[:, blk]
        vals, sels = [], []
        for k in range(TOP_K):
            m = jnp.max(work, axis=0, keepdims=True)
            idx = jnp.min(jnp.where(work == m, iota_e, n_exp), axis=0, keepdims=True)
            sel = iota_e == idx
            vals.append(m)
            sels.append(sel)
            idx_s[k:k + 1, blk] = idx
            work = jnp.where(sel, -jnp.inf, work)
        exps = [jnp.exp(v - vals[0]) for v in vals]
        den = exps[0]
        for e in exps[1:]:
            den = den + e
        chosen = jnp.zeros((n_exp, tb), f32)
        for k in range(TOP_K):
            w_ref[k:k + 1, blk] = exps[k] / den
            chosen = chosen + sels[k].astype(f32)
        earlier = jnp.dot(chosen.astype(bf16), before, preferred_element_type=f32) + seen
        for k in range(TOP_K):
            rank_s[k:k + 1, blk] = jnp.sum(jnp.where(sels[k], earlier, 0.0), axis=0, keepdims=True)
        return seen + jnp.sum(chosen, axis=1, keepdims=True)

    counts = lax.fori_loop(0, n_tok // tb, select, jnp.zeros((n_exp, 1), f32))

    padded = jnp.ceil(counts / granule) * granule
    ee_r = lax.broadcasted_iota(i32, (n_exp, n_exp), 0)
    ee_c = lax.broadcasted_iota(i32, (n_exp, n_exp), 1)
    padded_row = jnp.sum(jnp.where(ee_r == ee_c, padded, 0.0), axis=0, keepdims=True)
    starts = jnp.sum(jnp.where(ee_c < ee_r, padded_row, 0.0), axis=1, keepdims=True)

    def place(b, carry):
        blk = pl.ds(pl.multiple_of(b * tb, tb), tb)
        for k in range(TOP_K):
            sel = iota_e == idx_s[k:k + 1, blk]
            start = jnp.sum(jnp.where(sel, starts, 0.0), axis=0, keepdims=True)
            pos_ref[k:k + 1, blk] = (start + rank_s[k:k + 1, blk]).astype(i32)
        return carry

    lax.fori_loop(0, n_tok // tb, place, 0)
    cnt_ref[...] = jnp.broadcast_to(counts, cnt_ref.shape).astype(i32)


def _route(logits_t, plan):
    n_exp, n_tok = logits_t.shape
    kern = functools.partial(_route_kernel, tb=plan.tb_route, granule=float(plan.tm_moe))
    return pl.pallas_call(
        kern,
        out_shape=(jax.ShapeDtypeStruct((TOP_K, n_tok), f32),
                   jax.ShapeDtypeStruct((TOP_K, n_tok), i32),
                   jax.ShapeDtypeStruct((n_exp, LANES), i32)),
        scratch_shapes=[pltpu.VMEM((TOP_K, n_tok), i32), pltpu.VMEM((TOP_K, n_tok), f32)],
        compiler_params=pltpu.CompilerParams(vmem_limit_bytes=VMEM_LIMIT),
        name="route",
    )(logits_t)


def _dispatch_kernel(pstart_ref, pend_ref, nv_ref, pos_ref, h_ref, xs_ref, zrows, sem, zsem, *, tc, tm):
    c = pl.program_id(0)
    n_exp = pstart_ref.shape[0]
    zr = zrows.shape[0]

    def issue(j, carry):
        for k in range(TOP_K):
            pltpu.make_async_copy(h_ref.at[pl.ds(j, 1)], xs_ref.at[pl.ds(pos_ref[0, k * tc + j], 1)], sem).start()
        return carry
    lax.fori_loop(0, tc, issue, 0)

    step_rows = xs_ref.at[pl.ds(0, TOP_K * tc)]
    pltpu.make_async_copy(step_rows, step_rows, sem).wait()

    @pl.when(c == pl.num_programs(0) - 1)
    def _():
        zrows[...] = jnp.zeros_like(zrows)

        def row_fill(p):
            return pltpu.make_async_copy(zrows.at[pl.ds(0, 1)], xs_ref.at[pl.ds(p, 1)], zsem)

        def chunk_fill(q):
            return pltpu.make_async_copy(zrows, xs_ref.at[pl.ds(pl.multiple_of(q * zr, zr), zr)], zsem)

        def fill(lo, hi):
            mid = jnp.minimum((lo + (zr - 1)) // zr * zr, hi)
            lax.fori_loop(lo, mid, lambda p, a: (row_fill(p).start(), a)[1], 0)
            lax.fori_loop(lo, mid, lambda p, a: (row_fill(0).wait(), a)[1], 0)
            lax.fori_loop(mid // zr, hi // zr, lambda q, a: (chunk_fill(q).start(), a)[1], 0)
            lax.fori_loop(mid // zr, hi // zr, lambda q, a: (chunk_fill(0).wait(), a)[1], 0)

        def per_expert(e, carry):
            fill(pstart_ref[e], pend_ref[e])
            return carry
        lax.fori_loop(0, n_exp, per_expert, 0)
        fill(nv_ref[0] * tm, xs_ref.shape[0])


def _dispatch(h2, pos_blocks, pad_start, pad_end, n_valid, n_rows, plan):
    n_tok, dw = h2.shape
    tc = plan.tc_disp
    zr = min(plan.sub_moe, 128)
    assert plan.tm_moe % zr == 0
    kern = functools.partial(_dispatch_kernel, tc=tc, tm=plan.tm_moe)
    return pl.pallas_call(
        kern,
        out_shape=jax.ShapeDtypeStruct((n_rows, dw), h2.dtype),
        grid_spec=pltpu.PrefetchScalarGridSpec(
            num_scalar_prefetch=3,
            grid=(n_tok // tc,),
            in_specs=[
                pl.BlockSpec((None, 1, TOP_K * tc), lambda c, *_: (c, 0, 0), memory_space=pltpu.SMEM),
                pl.BlockSpec((tc, dw), lambda c, *_: (c, 0)),
            ],
            out_specs=pl.BlockSpec(memory_space=pl.ANY),
            scratch_shapes=[pltpu.VMEM((zr, dw), h2.dtype), pltpu.SemaphoreType.DMA, pltpu.SemaphoreType.DMA],
        ),
        compiler_params=_params("arbitrary"),
        name="dispatch",
    )(pad_start, pad_end, n_valid, pos_blocks, h2)


def _moe_kernel(te_ref, ns_ref, nv_ref, x_ref, wg_ref, bg_ref, wu_ref, bu_ref, wd_ref, bd_ref, o_ref,
                xb, act, *, n_up, tf, sub, nsub_max):
    i = pl.program_id(0)
    j = pl.program_id(1)
    valid = i < nv_ref[0]
    nsub = ns_ref[jnp.minimum(i, nv_ref[0] - 1)]
    half = x_ref.shape[1]

    @pl.when(jnp.logical_and(valid, j == 0))
    def _():
        lo, hi = _unpack_halves(x_ref[...])
        xb[:, :half] = lo
        xb[:, half:] = hi

    @pl.when(jnp.logical_and(valid, j < n_up))
    def _():
        cols = pl.ds(pl.multiple_of(j * tf, tf), tf)
        for n in range(1, nsub_max + 1):
            @pl.when(nsub == n)
            def _(n=n):
                x = xb[0:n * sub, :]
                g = jnp.dot(x, wg_ref[...].astype(bf16), preferred_element_type=f32) + bg_ref[...]
                u = jnp.dot(x, wu_ref[...].astype(bf16), preferred_element_type=f32) + bu_ref[...]
                g = jnp.minimum(g, SWIGLU_LIMIT)
                u = jnp.clip(u, -SWIGLU_LIMIT, SWIGLU_LIMIT)
                a = g * jax.nn.sigmoid(SWIGLU_ALPHA * g) * (u + 1.0)
                act[0:n * sub, cols] = a.astype(bf16)

    @pl.when(jnp.logical_and(valid, j >= n_up))
    def _():
        for n in range(1, nsub_max + 1):
            @pl.when(nsub == n)
            def _(n=n):
                o_ref[0:n * sub, :] = jnp.dot(act[0:n * sub, :], wd_ref[...].astype(bf16),
                                              preferred_element_type=f32) + bd_ref[...]
                if n < nsub_max:
                    o_ref[n * sub:, :] = jnp.zeros((o_ref.shape[0] - n * sub, o_ref.shape[1]), f32)

    @pl.when(jnp.logical_and(jnp.logical_not(valid), j >= n_up))
    def _():
        o_ref[...] = jnp.zeros_like(o_ref)


def _moe(xs, tile_expert, tile_nsub, n_valid, w_gate, b_gate, w_up, b_up, w_down, b_down, layer, plan):
    n_rows, half = xs.shape
    d = 2 * half
    n_layer, n_exp, _, f = w_gate.shape
    tm, tf, td = plan.tm_moe, plan.tf_moe, plan.td_moe
    n_up, n_down = f // tf, d // td
    n_tile = n_rows // tm

    def tile(i, nv):
        return jnp.minimum(i, nv[0] - 1)

    def phase(i, j, nv):
        return jnp.where(i < nv[0], j, n_up + n_down - 1)

    def up_idx(i, j, te, ns, nv):
        return (layer, te[tile(i, nv)], 0, jnp.minimum(phase(i, j, nv), n_up - 1))

    def down_idx(i, j, te, ns, nv):
        return (layer, te[tile(i, nv)], 0, jnp.maximum(phase(i, j, nv) - n_up, 0))

    kern = functools.partial(_moe_kernel, n_up=n_up, tf=tf, sub=plan.sub_moe, nsub_max=plan.nsub_moe)
    return pl.pallas_call(
        kern,
        out_shape=jax.ShapeDtypeStruct((n_rows, d), f32),
        grid_spec=pltpu.PrefetchScalarGridSpec(
            num_scalar_prefetch=3,
            grid=(n_tile, n_up + n_down),
            in_specs=[
                pl.BlockSpec((tm, half), lambda i, j, te, ns, nv: (tile(i, nv), 0)),
                pl.BlockSpec((None, None, d, tf), up_idx),
                pl.BlockSpec((None, None, 1, tf), up_idx),
                pl.BlockSpec((None, None, d, tf), up_idx),
                pl.BlockSpec((None, None, 1, tf), up_idx),
                pl.BlockSpec((None, None, f, td), down_idx),
                pl.BlockSpec((None, None, 1, td), down_idx),
            ],
            out_specs=pl.BlockSpec((tm, td), lambda i, j, te, ns, nv: (i, jnp.maximum(j - n_up, 0))),
            scratch_shapes=[pltpu.VMEM((tm, d), bf16), pltpu.VMEM((tm, f), bf16)],
        ),
        compiler_params=_params("arbitrary", "arbitrary"),
        name="moe",
    )(tile_expert, tile_nsub, n_valid, xs, w_gate, b_gate.reshape(n_layer, n_exp, 1, f), w_up,
      b_up.reshape(n_layer, n_exp, 1, f), w_down, b_down.reshape(n_layer, n_exp, 1, d))


def _combine_kernel(pos_ref, nxt_ref, ys_ref, w_ref, x1_ref, gf_ref, g_ref, b_ref, o_ref, buf, sem,
                    *, tc, alpha):
    i = pl.program_id(0)
    n = pl.num_programs(0)
    slot = i % 2
    rows = TOP_K * tc

    def gather(p_ref, s):
        def issue(j, carry):
            for k in range(TOP_K):
                pltpu.make_async_copy(ys_ref.at[pl.ds(p_ref[0, k * tc + j], 1)],
                                      buf.at[s, pl.ds(k * tc + j, 1)], sem.at[s]).start()
            return carry
        lax.fori_loop(0, tc, issue, 0)

    @pl.when(i == 0)
    def _():
        gather(pos_ref, 0)

    @pl.when(i + 1 < n)
    def _():
        gather(nxt_ref, 1 - slot)

    pltpu.make_async_copy(ys_ref.at[pl.ds(0, rows)], buf.at[slot], sem.at[slot]).wait()

    f = w_ref[:, 0:1] * buf[slot, 0:tc, :]
    for k in range(1, TOP_K):
        f = f + w_ref[:, k:k + 1] * buf[slot, k * tc:(k + 1) * tc, :]
    o_ref[...] = _layernorm(alpha * x1_ref[...] + gf_ref[...] * f) * g_ref[...] + b_ref[...]


def _combine(ys, pos_blocks, top_w, x1, gf, ln_g, ln_b, seq, alpha, plan):
    n_tok, d = x1.shape
    tc = plan.tc_comb
    n_step = n_tok // tc
    per_seq = seq // tc
    kern = functools.partial(_combine_kernel, tc=tc, alpha=alpha)
    smem = functools.partial(pl.BlockSpec, (None, 1, TOP_K * tc), memory_space=pltpu.SMEM)
    return pl.pallas_call(
        kern,
        out_shape=jax.ShapeDtypeStruct((n_tok, d), f32),
        grid=(n_step,),
        in_specs=[
            smem(lambda i: (i, 0, 0)),
            smem(lambda i: (jnp.minimum(i + 1, n_step - 1), 0, 0)),
            pl.BlockSpec(memory_space=pl.ANY),
            pl.BlockSpec((tc, TOP_K), lambda i: (i, 0)),
            pl.BlockSpec((tc, d), lambda i: (i, 0)),
            pl.BlockSpec((None, 1, d), lambda i: (i // per_seq, 0, 0)),
            pl.BlockSpec((1, d), lambda i: (0, 0)),
            pl.BlockSpec((1, d), lambda i: (0, 0)),
        ],
        out_specs=pl.BlockSpec((tc, d), lambda i: (i, 0)),
        scratch_shapes=[pltpu.VMEM((2, TOP_K * tc, d), f32), pltpu.SemaphoreType.DMA((2,))],
        compiler_params=_params("arbitrary"),
        name="combine",
    )(pos_blocks, pos_blocks, ys, top_w, x1, gf, ln_g.reshape(1, d), ln_b.reshape(1, d))


def _pos_blocks(pos, tc):
    k, n_tok = pos.shape
    return pos.reshape(k, n_tok // tc, tc).transpose(1, 0, 2).reshape(n_tok // tc, 1, k * tc)


def _tile_table(counts, tm, sub, n_tile):
    n_exp = counts.shape[0]
    tiles = (counts + tm - 1) // tm
    ends = jnp.cumsum(tiles)
    first = ends - tiles
    t = jnp.arange(n_tile, dtype=i32)
    tile_expert = jnp.minimum(jnp.sum(t[:, None] >= ends[None, :], axis=1), n_exp - 1).astype(i32)
    rows = jnp.clip(counts[tile_expert] - (t - first[tile_expert]) * tm, 0, tm)
    tile_nsub = ((rows + sub - 1) // sub).astype(i32)
    pad_start = (first * tm + counts).astype(i32)
    pad_end = (ends * tm).astype(i32)
    return tile_expert, tile_nsub, ends[-1:].astype(i32), pad_start, pad_end


def kernel(x, c, w_ada, b_ada, w_in, b_in, w_pool, b_pool, pool_scale, w_dw, b_dw, conv_ln_g, conv_ln_b,
           w_out, b_out, ln1_g, ln1_b, w_router, b_router, w_gate, b_gate, w_up, b_up, w_down, b_down,
           ln2_g, ln2_b):
    batch, seq, d = x.shape
    depth = w_ada.shape[0]
    n_exp, f = w_gate.shape[1], w_gate.shape[3]
    n_tok = batch * seq
    alpha = (2.0 * depth) ** 0.25
    plan = _plan(seq, d, n_tok, f)
    n_tile = n_tok * TOP_K // plan.tm_moe + n_exp
    n_rows = n_tile * plan.tm_moe

    mod = _ada(c, w_ada, b_ada, plan)
    xt = x.reshape(n_tok, d)
    for l in range(depth):
        sh_m, sc_m, g_m, sh_f, sc_f, g_f = [m.reshape(batch, 1, d) for m in jnp.split(mod[l], N_MOD, axis=-1)]
        row = lambda a: a[l].reshape(1, -1)
        lw = dict(
            w_pool=w_pool[l].astype(bf16), b_pool=row(b_pool), pool_scale=row(pool_scale),
            w_dw=w_dw[l], b_dw=row(b_dw), conv_ln_g=row(conv_ln_g), conv_ln_b=row(conv_ln_b),
            w_out=w_out[l].astype(bf16), b_out=row(b_out), ln1_g=row(ln1_g), ln1_b=row(ln1_b),
            w_router=w_router[l].astype(bf16), b_router=row(b_router))
        z = _inproj(xt, sh_m, sc_m, w_in[l].astype(bf16), b_in[l], seq, plan)
        x1, h2, logits = _mix(z, xt, g_m, sh_f, sc_f, lw, seq, alpha, plan)
        top_w, pos, counts = _route(logits.T, plan)
        tile_expert, tile_nsub, n_valid, pad_start, pad_end = _tile_table(
            counts[:, 0], plan.tm_moe, plan.sub_moe, n_tile)
        xs = _dispatch(h2, _pos_blocks(pos, plan.tc_disp), pad_start, pad_end, n_valid, n_rows, plan)
        ys = _moe(xs, tile_expert, tile_nsub, n_valid, w_gate, b_gate, w_up, b_up, w_down, b_down, l, plan)
        xt = _combine(ys, _pos_blocks(pos, plan.tc_comb), top_w.T, x1, g_f, ln2_g[l], ln2_b[l], seq, alpha, plan)
    return xt.reshape(batch, seq, d)
```

```python
import functools
from typing import NamedTuple

import jax
import jax.numpy as jnp
from jax import lax
from jax.experimental import pallas as pl
from jax.experimental.pallas import tpu as pltpu

POOL_WINDOWS = (2, 4, 8, 16)
CONV_KERNEL = 31
TOP_K = 4
SWIGLU_LIMIT = 7.0
SWIGLU_ALPHA = 1.702
LN_EPS = 1e-5
N_MOD = 6

HALO = 32
LANES = 128
SUBLANES = 8
ISSUE_UNROLL = 4
VMEM_LIMIT = 56 * 1024 * 1024

f32 = jnp.float32
bf16 = jnp.bfloat16
i32 = jnp.int32
u32 = jnp.uint32


class Plan(NamedTuple):
    tn_ada: int
    tm_in: int
    tm_mix: int
    conv_rows: int
    tb_route: int
    sub_moe: int
    nsub_moe: int
    tf_moe: int
    td_moe: int
    tc_disp: int
    tc_comb: int

    @property
    def tm_moe(self):
        return self.sub_moe * self.nsub_moe


def _plan(seq, d_model, n_tok, d_expert):
    def fit(pref, n):
        t = min(pref, n)
        assert n % t == 0, (pref, n)
        return t
    return Plan(
        tn_ada=fit(1024, N_MOD * d_model),
        tm_in=fit(512, seq),
        tm_mix=fit(256, seq),
        conv_rows=64,
        tb_route=fit(512, n_tok),
        sub_moe=256,
        nsub_moe=5,
        tf_moe=fit(512, d_expert),
        td_moe=fit(512, d_model),
        tc_disp=fit(1024, n_tok),
        tc_comb=fit(128, n_tok),
    )


def _layernorm(x):
    mu = jnp.mean(x, axis=-1, keepdims=True)
    xc = x - mu
    var = jnp.mean(xc * xc, axis=-1, keepdims=True)
    return xc * lax.rsqrt(var + LN_EPS)


def _silu(x):
    return x * jax.nn.sigmoid(x)


def _pack_halves(xb):
    half = xb.shape[1] // 2
    lo = lax.bitcast_convert_type(xb[:, :half].astype(f32), u32) >> 16
    hi = lax.bitcast_convert_type(xb[:, half:].astype(f32), u32) & jnp.uint32(0xFFFF0000)
    return lo | hi


def _unpack_halves(w):
    lo = lax.bitcast_convert_type(w << 16, f32).astype(bf16)
    hi = lax.bitcast_convert_type(w & jnp.uint32(0xFFFF0000), f32).astype(bf16)
    return lo, hi


def _params(*sem):
    return pltpu.CompilerParams(dimension_semantics=sem, vmem_limit_bytes=VMEM_LIMIT)


def _ada_kernel(c_ref, w_ref, b_ref, o_ref):
    c = c_ref[...]
    o_ref[...] = jnp.dot(_silu(c).astype(bf16), w_ref[...].astype(bf16),
                         preferred_element_type=f32) + b_ref[...]


def _ada(c, w_ada, b_ada, plan):
    n_layer, d, n = w_ada.shape
    b = c.shape[0]
    tn = plan.tn_ada
    return pl.pallas_call(
        _ada_kernel,
        out_shape=jax.ShapeDtypeStruct((n_layer, b, n), f32),
        grid=(n_layer, n // tn),
        in_specs=[
            pl.BlockSpec((b, d), lambda l, j: (0, 0)),
            pl.BlockSpec((None, d, tn), lambda l, j: (l, 0, j)),
            pl.BlockSpec((None, 1, tn), lambda l, j: (l, 0, j)),
        ],
        out_specs=pl.BlockSpec((None, b, tn), lambda l, j: (l, 0, j)),
        compiler_params=_params("arbitrary", "arbitrary"),
        name="ada",
    )(c, w_ada, b_ada.reshape(n_layer, 1, n))


def _inproj_kernel(x_ref, sh_ref, sc_ref, w_ref, b_ref, z_ref):
    h = _layernorm(x_ref[...]) * (1.0 + sc_ref[...]) + sh_ref[...]
    z_ref[...] = jnp.dot(h.astype(bf16), w_ref[...], preferred_element_type=f32) + b_ref[...]


def _inproj(x, sh, sc, w_in, b_in, seq, plan):
    n_tok, d = x.shape
    n = w_in.shape[1]
    tm = plan.tm_in
    per_seq = seq // tm
    vec = pl.BlockSpec((None, 1, d), lambda i: (i // per_seq, 0, 0))
    return pl.pallas_call(
        _inproj_kernel,
        out_shape=jax.ShapeDtypeStruct((n_tok, n), f32),
        grid=(n_tok // tm,),
        in_specs=[
            pl.BlockSpec((tm, d), lambda i: (i, 0)),
            vec, vec,
            pl.BlockSpec((d, n), lambda i: (0, 0)),
            pl.BlockSpec((1, n), lambda i: (0, 0)),
        ],
        out_specs=pl.BlockSpec((tm, n), lambda i: (i, 0)),
        compiler_params=_params("arbitrary"),
        name="inproj",
    )(x, sh, sc, w_in, b_in.reshape(1, n))


def _mix_kernel(zc_ref, zp_ref, x_ref, gm_ref, shf_ref, scf_ref,
                wpool_ref, bpool_ref, pscale_ref, wdw_ref, bdw_ref, cg_ref, cb_ref,
                wout_ref, bout_ref, l1g_ref, l1b_ref, wr_ref, br_ref,
                x1_ref, h2_ref, lg_ref,
                zbuf, ubuf, cbuf, ycat, *, per_seq, alpha, conv_rows):
    tm, d = x_ref.shape
    pw = d // 2
    cw = d - pw
    grp = pw // len(POOL_WINDOWS)
    i = pl.program_id(0)
    tile_in_seq = i % per_seq
    first = tile_in_seq == 0

    zbuf[0:HALO, :] = jnp.where(first, 0.0, zp_ref[:, 0:pw])
    zbuf[HALO:, :] = zc_ref[:, 0:pw]
    ubuf[0:HALO, :] = jnp.where(first, 0.0, zp_ref[:, pw:pw + cw] * jax.nn.sigmoid(zp_ref[:, pw + cw:]))
    ubuf[HALO:, :] = zc_ref[:, pw:pw + cw] * jax.nn.sigmoid(zc_ref[:, pw + cw:])

    t_pos = (tile_in_seq * tm + lax.broadcasted_iota(i32, (tm, 1), 0) + 1).astype(f32)
    for g, w in enumerate(POOL_WINDOWS):
        cols = slice(g * grp, (g + 1) * grp)
        tok = zbuf[HALO:HALO + tm, cols]
        win = tok
        for k in range(1, w):
            win = win + zbuf[HALO - k:HALO - k + tm, cols]
        pooled = win / jnp.minimum(t_pos, float(w)) - tok
        ya = jnp.dot(pooled.astype(bf16), wpool_ref[g], preferred_element_type=f32) + bpool_ref[:, cols]
        ycat[:, cols] = (ya * pscale_ref[:, cols]).astype(bf16)

    lead = HALO - (CONV_KERNEL - 1)

    def conv_lanes(c, carry):
        lanes = pl.ds(pl.multiple_of(c * LANES, LANES), LANES)
        for r0 in range(0, tm, conv_rows):
            acc = bdw_ref[:, lanes]
            for shift in range(SUBLANES):
                taps = [k for k in range(CONV_KERNEL) if (lead + k) % SUBLANES == shift]
                if not taps:
                    continue
                span = conv_rows + (SUBLANES if shift else 0)
                part = None
                for k in taps:
                    row = r0 + (lead + k) - shift
                    term = wdw_ref[k:k + 1, lanes] * ubuf[row:row + span, lanes]
                    part = term if part is None else part + term
                acc = acc + part[shift:shift + conv_rows, :]
            cbuf[r0:r0 + conv_rows, lanes] = acc
        return carry
    lax.fori_loop(0, cw // LANES, conv_lanes, 0)
    yb = _silu(_layernorm(cbuf[...]) * cg_ref[...] + cb_ref[...])
    ycat[:, pw:] = yb.astype(bf16)

    y = jnp.dot(ycat[...], wout_ref[...], preferred_element_type=f32) + bout_ref[...]
    x1 = _layernorm(alpha * x_ref[...] + gm_ref[...] * y) * l1g_ref[...] + l1b_ref[...]
    x1_ref[...] = x1
    h2 = (_layernorm(x1) * (1.0 + scf_ref[...]) + shf_ref[...]).astype(bf16)
    h2_ref[...] = _pack_halves(h2)
    lg_ref[...] = jnp.dot(h2, wr_ref[...], preferred_element_type=f32) + br_ref[...]


def _mix(z, x, gm, shf, scf, lw, seq, alpha, plan):
    n_tok, d = x.shape
    nz = z.shape[1]
    n_exp = lw["w_router"].shape[1]
    pw = d // 2
    cw = d - pw
    tm = plan.tm_mix
    per_seq = seq // tm
    halo_per_tile = tm // HALO
    vec = pl.BlockSpec((None, 1, d), lambda i: (i // per_seq, 0, 0))

    def whole(a):
        return pl.BlockSpec(a.shape, lambda i: (0,) * a.ndim)

    weights = [lw["w_pool"], lw["b_pool"], lw["pool_scale"], lw["w_dw"], lw["b_dw"], lw["conv_ln_g"],
               lw["conv_ln_b"], lw["w_out"], lw["b_out"], lw["ln1_g"], lw["ln1_b"], lw["w_router"], lw["b_router"]]
    kern = functools.partial(_mix_kernel, per_seq=per_seq, alpha=alpha, conv_rows=min(plan.conv_rows, tm))
    return pl.pallas_call(
        kern,
        out_shape=(jax.ShapeDtypeStruct((n_tok, d), f32),
                   jax.ShapeDtypeStruct((n_tok, d // 2), u32),
                   jax.ShapeDtypeStruct((n_tok, n_exp), f32)),
        grid=(n_tok // tm,),
        in_specs=[
            pl.BlockSpec((tm, nz), lambda i: (i, 0)),
            pl.BlockSpec((HALO, nz), lambda i: (jnp.maximum(i * halo_per_tile - 1, 0), 0)),
            pl.BlockSpec((tm, d), lambda i: (i, 0)),
            vec, vec, vec,
        ] + [whole(a) for a in weights],
        out_specs=(pl.BlockSpec((tm, d), lambda i: (i, 0)),
                   pl.BlockSpec((tm, d // 2), lambda i: (i, 0)),
                   pl.BlockSpec((tm, n_exp), lambda i: (i, 0))),
        scratch_shapes=[pltpu.VMEM((tm + HALO, pw), f32), pltpu.VMEM((tm + HALO, cw), f32),
                        pltpu.VMEM((tm, cw), f32), pltpu.VMEM((tm, d), bf16)],
        compiler_params=_params("arbitrary"),
        name="mix",
    )(z, z, x, gm, shf, scf, *weights)


def _route_kernel(lg_ref, w_ref, pos_ref, cnt_ref, idx_s, rank_s, *, tb, granule):
    n_exp, n_tok = lg_ref.shape
    iota_e = lax.broadcasted_iota(i32, (n_exp, tb), 0)
    before = (lax.broadcasted_iota(i32, (tb, tb), 0) < lax.broadcasted_iota(i32, (tb, tb), 1)).astype(bf16)

    def select(b, seen):
        blk = pl.ds(pl.multiple_of(b * tb, tb), tb)
        work = lg_ref[:, blk]
        vals, sels = [], []
        for k in range(TOP_K):
            m = jnp.max(work, axis=0, keepdims=True)
            idx = jnp.min(jnp.where(work == m, iota_e, n_exp), axis=0, keepdims=True)
            sel = iota_e == idx
            vals.append(m)
            sels.append(sel)
            idx_s[k:k + 1, blk] = idx
            work = jnp.where(sel, -jnp.inf, work)
        exps = [jnp.exp(v - vals[0]) for v in vals]
        den = exps[0]
        for e in exps[1:]:
            den = den + e
        chosen = jnp.zeros((n_exp, tb), f32)
        for k in range(TOP_K):
            w_ref[k:k + 1, blk] = exps[k] / den
            chosen = chosen + sels[k].astype(f32)
        earlier = jnp.dot(chosen.astype(bf16), before, preferred_element_type=f32) + seen
        for k in range(TOP_K):
            rank_s[k:k + 1, blk] = jnp.sum(jnp.where(sels[k], earlier, 0.0), axis=0, keepdims=True)
        return seen + jnp.sum(chosen, axis=1, keepdims=True)

    counts = lax.fori_loop(0, n_tok // tb, select, jnp.zeros((n_exp, 1), f32))

    padded = jnp.ceil(counts / granule) * granule
    ee_r = lax.broadcasted_iota(i32, (n_exp, n_exp), 0)
    ee_c = lax.broadcasted_iota(i32, (n_exp, n_exp), 1)
    padded_row = jnp.sum(jnp.where(ee_r == ee_c, padded, 0.0), axis=0, keepdims=True)
    starts = jnp.sum(jnp.where(ee_c < ee_r, padded_row, 0.0), axis=1, keepdims=True)

    def place(b, carry):
        blk = pl.ds(pl.multiple_of(b * tb, tb), tb)
        for k in range(TOP_K):
            sel = iota_e == idx_s[k:k + 1, blk]
            start = jnp.sum(jnp.where(sel, starts, 0.0), axis=0, keepdims=True)
            pos_ref[k:k + 1, blk] = (start + rank_s[k:k + 1, blk]).astype(i32)
        return carry

    lax.fori_loop(0, n_tok // tb, place, 0)
    cnt_ref[...] = jnp.broadcast_to(counts, cnt_ref.shape).astype(i32)


def _route(logits_t, plan):
    n_exp, n_tok = logits_t.shape
    kern = functools.partial(_route_kernel, tb=plan.tb_route, granule=float(plan.tm_moe))
    return pl.pallas_call(
        kern,
        out_shape=(jax.ShapeDtypeStruct((TOP_K, n_tok), f32),
                   jax.ShapeDtypeStruct((TOP_K, n_tok), i32),
                   jax.ShapeDtypeStruct((n_exp, LANES), i32)),
        scratch_shapes=[pltpu.VMEM((TOP_K, n_tok), i32), pltpu.VMEM((TOP_K, n_tok), f32)],
        compiler_params=pltpu.CompilerParams(vmem_limit_bytes=VMEM_LIMIT),
        name="route",
    )(logits_t)


def _dispatch_kernel(pstart_ref, pend_ref, nv_ref, pos_ref, h_ref, xs_ref, zrows, sem, zsem, *, tc, tm):
    c = pl.program_id(0)
    n_exp = pstart_ref.shape[0]
    zr = zrows.shape[0]

    def issue(j, carry):
        for k in range(TOP_K):
            pltpu.make_async_copy(h_ref.at[pl.ds(j, 1)], xs_ref.at[pl.ds(pos_ref[0, k * tc + j], 1)], sem).start()
        return carry
    lax.fori_loop(0, tc, issue, 0)

    step_rows = xs_ref.at[pl.ds(0, TOP_K * tc)]
    pltpu.make_async_copy(step_rows, step_rows, sem).wait()

    @pl.when(c == pl.num_programs(0) - 1)
    def _():
        zrows[...] = jnp.zeros_like(zrows)

        def row_fill(p):
            return pltpu.make_async_copy(zrows.at[pl.ds(0, 1)], xs_ref.at[pl.ds(p, 1)], zsem)

        def chunk_fill(q):
            return pltpu.make_async_copy(zrows, xs_ref.at[pl.ds(pl.multiple_of(q * zr, zr), zr)], zsem)

        def fill(lo, hi):
            mid = jnp.minimum((lo + (zr - 1)) // zr * zr, hi)
            lax.fori_loop(lo, mid, lambda p, a: (row_fill(p).start(), a)[1], 0)
            lax.fori_loop(lo, mid, lambda p, a: (row_fill(0).wait(), a)[1], 0)
            lax.fori_loop(mid // zr, hi // zr, lambda q, a: (chunk_fill(q).start(), a)[1], 0)
            lax.fori_loop(mid // zr, hi // zr, lambda q, a: (chunk_fill(0).wait(), a)[1], 0)

        def per_expert(e, carry):
            fill(pstart_ref[e], pend_ref[e])
            return carry
        lax.fori_loop(0, n_exp, per_expert, 0)
        fill(nv_ref[0] * tm, xs_ref.shape[0])


def _dispatch(h2, pos_blocks, pad_start, pad_end, n_valid, n_rows, plan):
    n_tok, dw = h2.shape
    tc = plan.tc_disp
    zr = min(plan.sub_moe, 128)
    assert plan.tm_moe % zr == 0
    kern = functools.partial(_dispatch_kernel, tc=tc, tm=plan.tm_moe)
    return pl.pallas_call(
        kern,
        out_shape=jax.ShapeDtypeStruct((n_rows, dw), h2.dtype),
        grid_spec=pltpu.PrefetchScalarGridSpec(
            num_scalar_prefetch=3,
            grid=(n_tok // tc,),
            in_specs=[
                pl.BlockSpec((None, 1, TOP_K * tc), lambda c, *_: (c, 0, 0), memory_space=pltpu.SMEM),
                pl.BlockSpec((tc, dw), lambda c, *_: (c, 0)),
            ],
            out_specs=pl.BlockSpec(memory_space=pl.ANY),
            scratch_shapes=[pltpu.VMEM((zr, dw), h2.dtype), pltpu.SemaphoreType.DMA, pltpu.SemaphoreType.DMA],
        ),
        compiler_params=_params("arbitrary"),
        name="dispatch",
    )(pad_start, pad_end, n_valid, pos_blocks, h2)


def _moe_kernel(te_ref, ns_ref, nv_ref, x_ref, wg_ref, bg_ref, wu_ref, bu_ref, wd_ref, bd_ref, o_ref,
                xb, act, *, n_up, tf, sub, nsub_max):
    i = pl.program_id(0)
    j = pl.program_id(1)
    valid = i < nv_ref[0]
    nsub = ns_ref[jnp.minimum(i, nv_ref[0] - 1)]
    half = x_ref.shape[1]

    @pl.when(jnp.logical_and(valid, j == 0))
    def _():
        lo, hi = _unpack_halves(x_ref[...])
        xb[:, :half] = lo
        xb[:, half:] = hi

    @pl.when(jnp.logical_and(valid, j < n_up))
    def _():
        cols = pl.ds(pl.multiple_of(j * tf, tf), tf)
        for n in range(1, nsub_max + 1):
            @pl.when(nsub == n)
            def _(n=n):
                x = xb[0:n * sub, :]
                g = jnp.dot(x, wg_ref[...].astype(bf16), preferred_element_type=f32) + bg_ref[...]
                u = jnp.dot(x, wu_ref[...].astype(bf16), preferred_element_type=f32) + bu_ref[...]
                g = jnp.minimum(g, SWIGLU_LIMIT)
                u = jnp.clip(u, -SWIGLU_LIMIT, SWIGLU_LIMIT)
                a = g * jax.nn.sigmoid(SWIGLU_ALPHA * g) * (u + 1.0)
                act[0:n * sub, cols] = a.astype(bf16)

    @pl.when(jnp.logical_and(valid, j >= n_up))
    def _():
        for n in range(1, nsub_max + 1):
            @pl.when(nsub == n)
            def _(n=n):
                o_ref[0:n * sub, :] = jnp.dot(act[0:n * sub, :], wd_ref[...].astype(bf16),
                                              preferred_element_type=f32) + bd_ref[...]
                if n < nsub_max:
                    o_ref[n * sub:, :] = jnp.zeros((o_ref.shape[0] - n * sub, o_ref.shape[1]), f32)

    @pl.when(jnp.logical_and(jnp.logical_not(valid), j >= n_up))
    def _():
        o_ref[...] = jnp.zeros_like(o_ref)


def _moe(xs, tile_expert, tile_nsub, n_valid, w_gate, b_gate, w_up, b_up, w_down, b_down, layer, plan):
    n_rows, half = xs.shape
    d = 2 * half
    n_layer, n_exp, _, f = w_gate.shape
    tm, tf, td = plan.tm_moe, plan.tf_moe, plan.td_moe
    n_up, n_down = f // tf, d // td
    n_tile = n_rows // tm

    def tile(i, nv):
        return jnp.minimum(i, nv[0] - 1)

    def phase(i, j, nv):
        return jnp.where(i < nv[0], j, n_up + n_down - 1)

    def up_idx(i, j, te, ns, nv):
        return (layer, te[tile(i, nv)], 0, jnp.minimum(phase(i, j, nv), n_up - 1))

    def down_idx(i, j, te, ns, nv):
        return (layer, te[tile(i, nv)], 0, jnp.maximum(phase(i, j, nv) - n_up, 0))

    kern = functools.partial(_moe_kernel, n_up=n_up, tf=tf, sub=plan.sub_moe, nsub_max=plan.nsub_moe)
    return pl.pallas_call(
        kern,
        out_shape=jax.ShapeDtypeStruct((n_rows, d), f32),
        grid_spec=pltpu.PrefetchScalarGridSpec(
            num_scalar_prefetch=3,
            grid=(n_tile, n_up + n_down),
            in_specs=[
                pl.BlockSpec((tm, half), lambda i, j, te, ns, nv: (tile(i, nv), 0), pipeline_mode=pl.Buffered(1)),
                pl.BlockSpec((None, None, d, tf), up_idx),
                pl.BlockSpec((None, None, 1, tf), up_idx),
                pl.BlockSpec((None, None, d, tf), up_idx),
                pl.BlockSpec((None, None, 1, tf), up_idx),
                pl.BlockSpec((None, None, f, td), down_idx),
                pl.BlockSpec((None, None, 1, td), down_idx),
            ],
            out_specs=pl.BlockSpec((tm, td), lambda i, j, te, ns, nv: (i, jnp.maximum(j - n_up, 0))),
            scratch_shapes=[pltpu.VMEM((tm, d), bf16), pltpu.VMEM((tm, f), bf16)],
        ),
        compiler_params=_params("arbitrary", "arbitrary"),
        name="moe",
    )(tile_expert, tile_nsub, n_valid, xs, w_gate, b_gate.reshape(n_layer, n_exp, 1, f), w_up,
      b_up.reshape(n_layer, n_exp, 1, f), w_down, b_down.reshape(n_layer, n_exp, 1, d))


def _combine_kernel(pos_ref, nxt_ref, ys_ref, w_ref, x1_ref, gf_ref, g_ref, b_ref, o_ref, buf, sem,
                    *, tc, alpha):
    i = pl.program_id(0)
    n = pl.num_programs(0)
    slot = i % 2
    rows = TOP_K * tc

    def gather(p_ref, s):
        def issue(j, carry):
            for k in range(TOP_K):
                pltpu.make_async_copy(ys_ref.at[pl.ds(p_ref[0, k * tc + j], 1)],
                                      buf.at[s, pl.ds(k * tc + j, 1)], sem.at[s]).start()
            return carry
        lax.fori_loop(0, tc, issue, 0, unroll=ISSUE_UNROLL)

    @pl.when(i == 0)
    def _():
        gather(pos_ref, 0)

    for s in (0, 1):
        @pl.when(jnp.logical_and(i + 1 < n, slot != s))
        def _(s=s):
            gather(nxt_ref, s)

    for s in (0, 1):
        @pl.when(slot == s)
        def _(s=s):
            pltpu.make_async_copy(ys_ref.at[pl.ds(0, rows)], buf.at[s], sem.at[s]).wait()
            f = w_ref[:, 0:1] * buf[s, 0:tc, :]
            for k in range(1, TOP_K):
                f = f + w_ref[:, k:k + 1] * buf[s, k * tc:(k + 1) * tc, :]
            o_ref[...] = _layernorm(alpha * x1_ref[...] + gf_ref[...] * f) * g_ref[...] + b_ref[...]


def _combine(ys, pos_blocks, top_w, x1, gf, ln_g, ln_b, seq, alpha, plan):
    n_tok, d = x1.shape
    tc = plan.tc_comb
    n_step = n_tok // tc
    per_seq = seq // tc
    kern = functools.partial(_combine_kernel, tc=tc, alpha=alpha)
    smem = functools.partial(pl.BlockSpec, (None, 1, TOP_K * tc), memory_space=pltpu.SMEM)
    return pl.pallas_call(
        kern,
        out_shape=jax.ShapeDtypeStruct((n_tok, d), f32),
        grid=(n_step,),
        in_specs=[
            smem(lambda i: (i, 0, 0)),
            smem(lambda i: (jnp.minimum(i + 1, n_step - 1), 0, 0)),
            pl.BlockSpec(memory_space=pl.ANY),
            pl.BlockSpec((tc, TOP_K), lambda i: (i, 0)),
            pl.BlockSpec((tc, d), lambda i: (i, 0)),
            pl.BlockSpec((None, 1, d), lambda i: (i // per_seq, 0, 0)),
            pl.BlockSpec((1, d), lambda i: (0, 0)),
            pl.BlockSpec((1, d), lambda i: (0, 0)),
        ],
        out_specs=pl.BlockSpec((tc, d), lambda i: (i, 0)),
        scratch_shapes=[pltpu.VMEM((2, TOP_K * tc, d), f32), pltpu.SemaphoreType.DMA((2,))],
        compiler_params=_params("arbitrary"),
        name="combine",
    )(pos_blocks, pos_blocks, ys, top_w, x1, gf, ln_g.reshape(1, d), ln_b.reshape(1, d))


def _pos_blocks(pos, tc):
    k, n_tok = pos.shape
    return pos.reshape(k, n_tok // tc, tc).transpose(1, 0, 2).reshape(n_tok // tc, 1, k * tc)


def _tile_table(counts, tm, sub, n_tile):
    n_exp = counts.shape[0]
    tiles = (counts + tm - 1) // tm
    ends = jnp.cumsum(tiles)
    first = ends - tiles
    t = jnp.arange(n_tile, dtype=i32)
    tile_expert = jnp.minimum(jnp.sum(t[:, None] >= ends[None, :], axis=1), n_exp - 1).astype(i32)
    rows = jnp.clip(counts[tile_expert] - (t - first[tile_expert]) * tm, 0, tm)
    tile_nsub = ((rows + sub - 1) // sub).astype(i32)
    pad_start = (first * tm + counts).astype(i32)
    pad_end = (ends * tm).astype(i32)
    return tile_expert, tile_nsub, ends[-1:].astype(i32), pad_start, pad_end


def kernel(x, c, w_ada, b_ada, w_in, b_in, w_pool, b_pool, pool_scale, w_dw, b_dw, conv_ln_g, conv_ln_b,
           w_out, b_out, ln1_g, ln1_b, w_router, b_router, w_gate, b_gate, w_up, b_up, w_down, b_down,
           ln2_g, ln2_b):
    batch, seq, d = x.shape
    depth = w_ada.shape[0]
    n_exp, f = w_gate.shape[1], w_gate.shape[3]
    n_tok = batch * seq
    alpha = (2.0 * depth) ** 0.25
    plan = _plan(seq, d, n_tok, f)
    n_tile = n_tok * TOP_K // plan.tm_moe + n_exp
    n_rows = n_tile * plan.tm_moe

    mod = _ada(c, w_ada, b_ada, plan)
    xt = x.reshape(n_tok, d)
    for l in range(depth):
        sh_m, sc_m, g_m, sh_f, sc_f, g_f = [m.reshape(batch, 1, d) for m in jnp.split(mod[l], N_MOD, axis=-1)]
        row = lambda a: a[l].reshape(1, -1)
        lw = dict(
            w_pool=w_pool[l].astype(bf16), b_pool=row(b_pool), pool_scale=row(pool_scale),
            w_dw=w_dw[l], b_dw=row(b_dw), conv_ln_g=row(conv_ln_g), conv_ln_b=row(conv_ln_b),
            w_out=w_out[l].astype(bf16), b_out=row(b_out), ln1_g=row(ln1_g), ln1_b=row(ln1_b),
            w_router=w_router[l].astype(bf16), b_router=row(b_router))
        z = _inproj(xt, sh_m, sc_m, w_in[l].astype(bf16), b_in[l], seq, plan)
        x1, h2, logits = _mix(z, xt, g_m, sh_f, sc_f, lw, seq, alpha, plan)
        top_w, pos, counts = _route(logits.T, plan)
        tile_expert, tile_nsub, n_valid, pad_start, pad_end = _tile_table(
            counts[:, 0], plan.tm_moe, plan.sub_moe, n_tile)
        xs = _dispatch(h2, _pos_blocks(pos, plan.tc_disp), pad_start, pad_end, n_valid, n_rows, plan)
        ys = _moe(xs, tile_expert, tile_nsub, n_valid, w_gate, b_gate, w_up, b_up, w_down, b_down, l, plan)
        xt = _combine(ys, _pos_blocks(pos, plan.tc_comb), top_w.T, x1, g_f, ln2_g[l], ln2_b[l], seq, alpha, plan)
    return xt.reshape(batch, seq, d)
```

```python
import functools
from typing import NamedTuple

import jax
import jax.numpy as jnp
from jax import lax
from jax.experimental import pallas as pl
from jax.experimental.pallas import tpu as pltpu

POOL_WINDOWS = (2, 4, 8, 16)
CONV_KERNEL = 31
TOP_K = 4
SWIGLU_LIMIT = 7.0
SWIGLU_ALPHA = 1.702
LN_EPS = 1e-5
N_MOD = 6

HALO = 32
LANES = 128
SUBLANES = 8
ISSUE_UNROLL = 4
VMEM_LIMIT = 58 * 1024 * 1024

f32 = jnp.float32
bf16 = jnp.bfloat16
i32 = jnp.int32
u32 = jnp.uint32


class Plan(NamedTuple):
    tn_ada: int
    tm_in: int
    tm_mix: int
    conv_rows: int
    tb_route: int
    sub_moe: int
    nsub_moe: int
    tf_moe: int
    td_moe: int
    tc_disp: int
    tc_comb: int

    @property
    def tm_moe(self):
        return self.sub_moe * self.nsub_moe


def _plan(seq, d_model, n_tok, d_expert):
    def fit(pref, n):
        t = min(pref, n)
        assert n % t == 0, (pref, n)
        return t
    return Plan(
        tn_ada=fit(1024, N_MOD * d_model),
        tm_in=fit(512, seq),
        tm_mix=fit(256, seq),
        conv_rows=64,
        tb_route=fit(512, n_tok),
        sub_moe=256,
        nsub_moe=5,
        tf_moe=fit(512, d_expert),
        td_moe=fit(512, d_model),
        tc_disp=fit(1024, n_tok),
        tc_comb=fit(128, n_tok),
    )


def _layernorm(x):
    mu = jnp.mean(x, axis=-1, keepdims=True)
    xc = x - mu
    var = jnp.mean(xc * xc, axis=-1, keepdims=True)
    return xc * lax.rsqrt(var + LN_EPS)


def _silu(x):
    return x * jax.nn.sigmoid(x)


def _pack_halves(xb):
    half = xb.shape[1] // 2
    lo = lax.bitcast_convert_type(xb[:, :half].astype(f32), u32) >> 16
    hi = lax.bitcast_convert_type(xb[:, half:].astype(f32), u32) & jnp.uint32(0xFFFF0000)
    return lo | hi


def _unpack_halves(w):
    lo = lax.bitcast_convert_type(w << 16, f32).astype(bf16)
    hi = lax.bitcast_convert_type(w & jnp.uint32(0xFFFF0000), f32).astype(bf16)
    return lo, hi


def _params(*sem):
    return pltpu.CompilerParams(dimension_semantics=sem, vmem_limit_bytes=VMEM_LIMIT)


def _ada_kernel(c_ref, w_ref, b_ref, o_ref):
    c = c_ref[...]
    o_ref[...] = jnp.dot(_silu(c).astype(bf16), w_ref[...].astype(bf16),
                         preferred_element_type=f32) + b_ref[...]


def _ada(c, w_ada, b_ada, plan):
    n_layer, d, n = w_ada.shape
    b = c.shape[0]
    tn = plan.tn_ada
    return pl.pallas_call(
        _ada_kernel,
        out_shape=jax.ShapeDtypeStruct((n_layer, b, n), f32),
        grid=(n_layer, n // tn),
        in_specs=[
            pl.BlockSpec((b, d), lambda l, j: (0, 0)),
            pl.BlockSpec((None, d, tn), lambda l, j: (l, 0, j)),
            pl.BlockSpec((None, 1, tn), lambda l, j: (l, 0, j)),
        ],
        out_specs=pl.BlockSpec((None, b, tn), lambda l, j: (l, 0, j)),
        compiler_params=_params("arbitrary", "arbitrary"),
        name="ada",
    )(c, w_ada, b_ada.reshape(n_layer, 1, n))


def _inproj_kernel(x_ref, sh_ref, sc_ref, w_ref, b_ref, z_ref):
    h = _layernorm(x_ref[...]) * (1.0 + sc_ref[...]) + sh_ref[...]
    z_ref[...] = jnp.dot(h.astype(bf16), w_ref[...], preferred_element_type=f32) + b_ref[...]


def _inproj(x, sh, sc, w_in, b_in, seq, plan):
    n_tok, d = x.shape
    n = w_in.shape[1]
    tm = plan.tm_in
    per_seq = seq // tm
    vec = pl.BlockSpec((None, 1, d), lambda i: (i // per_seq, 0, 0))
    return pl.pallas_call(
        _inproj_kernel,
        out_shape=jax.ShapeDtypeStruct((n_tok, n), f32),
        grid=(n_tok // tm,),
        in_specs=[
            pl.BlockSpec((tm, d), lambda i: (i, 0)),
            vec, vec,
            pl.BlockSpec((d, n), lambda i: (0, 0)),
            pl.BlockSpec((1, n), lambda i: (0, 0)),
        ],
        out_specs=pl.BlockSpec((tm, n), lambda i: (i, 0)),
        compiler_params=_params("arbitrary"),
        name="inproj",
    )(x, sh, sc, w_in, b_in.reshape(1, n))


def _mix_kernel(zc_ref, zp_ref, x_ref, gm_ref, shf_ref, scf_ref,
                wpool_ref, bpool_ref, pscale_ref, wdw_ref, bdw_ref, cg_ref, cb_ref,
                wout_ref, bout_ref, l1g_ref, l1b_ref, wr_ref, br_ref,
                x1_ref, h2_ref, lg_ref,
                zbuf, ubuf, cbuf, ycat, *, per_seq, alpha, conv_rows):
    tm, d = x_ref.shape
    pw = d // 2
    cw = d - pw
    grp = pw // len(POOL_WINDOWS)
    i = pl.program_id(0)
    tile_in_seq = i % per_seq
    first = tile_in_seq == 0

    zbuf[0:HALO, :] = jnp.where(first, 0.0, zp_ref[:, 0:pw])
    zbuf[HALO:, :] = zc_ref[:, 0:pw]
    ubuf[0:HALO, :] = jnp.where(first, 0.0, zp_ref[:, pw:pw + cw] * jax.nn.sigmoid(zp_ref[:, pw + cw:]))
    ubuf[HALO:, :] = zc_ref[:, pw:pw + cw] * jax.nn.sigmoid(zc_ref[:, pw + cw:])

    t_pos = (tile_in_seq * tm + lax.broadcasted_iota(i32, (tm, 1), 0) + 1).astype(f32)
    for g, w in enumerate(POOL_WINDOWS):
        cols = slice(g * grp, (g + 1) * grp)
        tok = zbuf[HALO:HALO + tm, cols]
        win = tok
        for k in range(1, w):
            win = win + zbuf[HALO - k:HALO - k + tm, cols]
        pooled = win / jnp.minimum(t_pos, float(w)) - tok
        ya = jnp.dot(pooled.astype(bf16), wpool_ref[g], preferred_element_type=f32) + bpool_ref[:, cols]
        ycat[:, cols] = (ya * pscale_ref[:, cols]).astype(bf16)

    lead = HALO - (CONV_KERNEL - 1)

    def conv_lanes(c, carry):
        lanes = pl.ds(pl.multiple_of(c * LANES, LANES), LANES)
        for r0 in range(0, tm, conv_rows):
            acc = bdw_ref[:, lanes]
            for shift in range(SUBLANES):
                taps = [k for k in range(CONV_KERNEL) if (lead + k) % SUBLANES == shift]
                if not taps:
                    continue
                span = conv_rows + (SUBLANES if shift else 0)
                part = None
                for k in taps:
                    row = r0 + (lead + k) - shift
                    term = wdw_ref[k:k + 1, lanes] * ubuf[row:row + span, lanes]
                    part = term if part is None else part + term
                acc = acc + part[shift:shift + conv_rows, :]
            cbuf[r0:r0 + conv_rows, lanes] = acc
        return carry
    lax.fori_loop(0, cw // LANES, conv_lanes, 0)
    yb = _silu(_layernorm(cbuf[...]) * cg_ref[...] + cb_ref[...])
    ycat[:, pw:] = yb.astype(bf16)

    y = jnp.dot(ycat[...], wout_ref[...], preferred_element_type=f32) + bout_ref[...]
    x1 = _layernorm(alpha * x_ref[...] + gm_ref[...] * y) * l1g_ref[...] + l1b_ref[...]
    x1_ref[...] = x1
    h2 = (_layernorm(x1) * (1.0 + scf_ref[...]) + shf_ref[...]).astype(bf16)
    h2_ref[...] = _pack_halves(h2)
    lg_ref[...] = jnp.dot(h2, wr_ref[...], preferred_element_type=f32) + br_ref[...]


def _mix(z, x, gm, shf, scf, lw, seq, alpha, plan):
    n_tok, d = x.shape
    nz = z.shape[1]
    n_exp = lw["w_router"].shape[1]
    pw = d // 2
    cw = d - pw
    tm = plan.tm_mix
    per_seq = seq // tm
    halo_per_tile = tm // HALO
    vec = pl.BlockSpec((None, 1, d), lambda i: (i // per_seq, 0, 0))

    def whole(a):
        return pl.BlockSpec(a.shape, lambda i: (0,) * a.ndim)

    weights = [lw["w_pool"], lw["b_pool"], lw["pool_scale"], lw["w_dw"], lw["b_dw"], lw["conv_ln_g"],
               lw["conv_ln_b"], lw["w_out"], lw["b_out"], lw["ln1_g"], lw["ln1_b"], lw["w_router"], lw["b_router"]]
    kern = functools.partial(_mix_kernel, per_seq=per_seq, alpha=alpha, conv_rows=min(plan.conv_rows, tm))
    return pl.pallas_call(
        kern,
        out_shape=(jax.ShapeDtypeStruct((n_tok, d), f32),
                   jax.ShapeDtypeStruct((n_tok, d // 2), u32),
                   jax.ShapeDtypeStruct((n_tok, n_exp), f32)),
        grid=(n_tok // tm,),
        in_specs=[
            pl.BlockSpec((tm, nz), lambda i: (i, 0)),
            pl.BlockSpec((HALO, nz), lambda i: (jnp.maximum(i * halo_per_tile - 1, 0), 0)),
            pl.BlockSpec((tm, d), lambda i: (i, 0)),
            vec, vec, vec,
        ] + [whole(a) for a in weights],
        out_specs=(pl.BlockSpec((tm, d), lambda i: (i, 0)),
                   pl.BlockSpec((tm, d // 2), lambda i: (i, 0)),
                   pl.BlockSpec((tm, n_exp), lambda i: (i, 0))),
        scratch_shapes=[pltpu.VMEM((tm + HALO, pw), f32), pltpu.VMEM((tm + HALO, cw), f32),
                        pltpu.VMEM((tm, cw), f32), pltpu.VMEM((tm, d), bf16)],
        compiler_params=_params("arbitrary"),
        name="mix",
    )(z, z, x, gm, shf, scf, *weights)


def _route_kernel(lg_ref, w_ref, pos_ref, cnt_ref, idx_s, rank_s, *, tb, granule):
    n_exp, n_tok = lg_ref.shape
    iota_e = lax.broadcasted_iota(i32, (n_exp, tb), 0)
    before = (lax.broadcasted_iota(i32, (tb, tb), 0) < lax.broadcasted_iota(i32, (tb, tb), 1)).astype(bf16)

    def select(b, seen):
        blk = pl.ds(pl.multiple_of(b * tb, tb), tb)
        work = lg_ref[:, blk]
        vals, sels = [], []
        for k in range(TOP_K):
            m = jnp.max(work, axis=0, keepdims=True)
            idx = jnp.min(jnp.where(work == m, iota_e, n_exp), axis=0, keepdims=True)
            sel = iota_e == idx
            vals.append(m)
            sels.append(sel)
            idx_s[k:k + 1, blk] = idx
            work = jnp.where(sel, -jnp.inf, work)
        exps = [jnp.exp(v - vals[0]) for v in vals]
        den = exps[0]
        for e in exps[1:]:
            den = den + e
        chosen = jnp.zeros((n_exp, tb), f32)
        for k in range(TOP_K):
            w_ref[k:k + 1, blk] = exps[k] / den
            chosen = chosen + sels[k].astype(f32)
        earlier = jnp.dot(chosen.astype(bf16), before, preferred_element_type=f32) + seen
        for k in range(TOP_K):
            rank_s[k:k + 1, blk] = jnp.sum(jnp.where(sels[k], earlier, 0.0), axis=0, keepdims=True)
        return seen + jnp.sum(chosen, axis=1, keepdims=True)

    counts = lax.fori_loop(0, n_tok // tb, select, jnp.zeros((n_exp, 1), f32))

    padded = jnp.ceil(counts / granule) * granule
    ee_r = lax.broadcasted_iota(i32, (n_exp, n_exp), 0)
    ee_c = lax.broadcasted_iota(i32, (n_exp, n_exp), 1)
    padded_row = jnp.sum(jnp.where(ee_r == ee_c, padded, 0.0), axis=0, keepdims=True)
    starts = jnp.sum(jnp.where(ee_c < ee_r, padded_row, 0.0), axis=1, keepdims=True)

    def place(b, carry):
        blk = pl.ds(pl.multiple_of(b * tb, tb), tb)
        for k in range(TOP_K):
            sel = iota_e == idx_s[k:k + 1, blk]
            start = jnp.sum(jnp.where(sel, starts, 0.0), axis=0, keepdims=True)
            pos_ref[k:k + 1, blk] = (start + rank_s[k:k + 1, blk]).astype(i32)
        return carry

    lax.fori_loop(0, n_tok // tb, place, 0)
    cnt_ref[...] = jnp.broadcast_to(counts, cnt_ref.shape).astype(i32)


def _route(logits_t, plan):
    n_exp, n_tok = logits_t.shape
    kern = functools.partial(_route_kernel, tb=plan.tb_route, granule=float(plan.tm_moe))
    return pl.pallas_call(
        kern,
        out_shape=(jax.ShapeDtypeStruct((TOP_K, n_tok), f32),
                   jax.ShapeDtypeStruct((TOP_K, n_tok), i32),
                   jax.ShapeDtypeStruct((n_exp, LANES), i32)),
        scratch_shapes=[pltpu.VMEM((TOP_K, n_tok), i32), pltpu.VMEM((TOP_K, n_tok), f32)],
        compiler_params=pltpu.CompilerParams(vmem_limit_bytes=VMEM_LIMIT),
        name="route",
    )(logits_t)


def _dispatch_kernel(pstart_ref, pend_ref, nv_ref, pos_ref, h_ref, xs_ref, zrows, sem, zsem, *, tc, tm):
    c = pl.program_id(0)
    n_exp = pstart_ref.shape[0]
    zr = zrows.shape[0]

    def issue(j, carry):
        for k in range(TOP_K):
            pltpu.make_async_copy(h_ref.at[pl.ds(j, 1)], xs_ref.at[pl.ds(pos_ref[0, k * tc + j], 1)],
                                  sem).start(priority=k % 2)
        return carry
    lax.fori_loop(0, tc, issue, 0)

    step_rows = xs_ref.at[pl.ds(0, TOP_K * tc)]
    pltpu.make_async_copy(step_rows, step_rows, sem).wait()

    @pl.when(c == pl.num_programs(0) - 1)
    def _():
        zrows[...] = jnp.zeros_like(zrows)

        def row_fill(p):
            return pltpu.make_async_copy(zrows.at[pl.ds(0, 1)], xs_ref.at[pl.ds(p, 1)], zsem)

        def chunk_fill(q):
            return pltpu.make_async_copy(zrows, xs_ref.at[pl.ds(pl.multiple_of(q * zr, zr), zr)], zsem)

        def fill(lo, hi):
            mid = jnp.minimum((lo + (zr - 1)) // zr * zr, hi)
            lax.fori_loop(lo, mid, lambda p, a: (row_fill(p).start(), a)[1], 0)
            lax.fori_loop(lo, mid, lambda p, a: (row_fill(0).wait(), a)[1], 0)
            lax.fori_loop(mid // zr, hi // zr, lambda q, a: (chunk_fill(q).start(), a)[1], 0)
            lax.fori_loop(mid // zr, hi // zr, lambda q, a: (chunk_fill(0).wait(), a)[1], 0)

        def per_expert(e, carry):
            fill(pstart_ref[e], pend_ref[e])
            return carry
        lax.fori_loop(0, n_exp, per_expert, 0)
        fill(nv_ref[0] * tm, xs_ref.shape[0])


def _dispatch(h2, pos_blocks, pad_start, pad_end, n_valid, n_rows, plan):
    n_tok, dw = h2.shape
    tc = plan.tc_disp
    zr = min(plan.sub_moe, 128)
    assert plan.tm_moe % zr == 0
    kern = functools.partial(_dispatch_kernel, tc=tc, tm=plan.tm_moe)
    return pl.pallas_call(
        kern,
        out_shape=jax.ShapeDtypeStruct((n_rows, dw), h2.dtype),
        grid_spec=pltpu.PrefetchScalarGridSpec(
            num_scalar_prefetch=3,
            grid=(n_tok // tc,),
            in_specs=[
                pl.BlockSpec((None, 1, TOP_K * tc), lambda c, *_: (c, 0, 0), memory_space=pltpu.SMEM),
                pl.BlockSpec((tc, dw), lambda c, *_: (c, 0)),
            ],
            out_specs=pl.BlockSpec(memory_space=pl.ANY),
            scratch_shapes=[pltpu.VMEM((zr, dw), h2.dtype), pltpu.SemaphoreType.DMA, pltpu.SemaphoreType.DMA],
        ),
        compiler_params=_params("arbitrary"),
        name="dispatch",
    )(pad_start, pad_end, n_valid, pos_blocks, h2)


def _moe_kernel(te_ref, ns_ref, nv_ref, x_ref, wg_ref, bg_ref, wu_ref, bu_ref, wd_ref, bd_ref, o_ref,
                xb, act, *, n_up, tf, sub, nsub_max):
    i = pl.program_id(0)
    j = pl.program_id(1)
    valid = i < nv_ref[0]
    nsub = ns_ref[jnp.minimum(i, nv_ref[0] - 1)]
    half = x_ref.shape[1]

    @pl.when(jnp.logical_and(valid, j == 0))
    def _():
        lo, hi = _unpack_halves(x_ref[...])
        xb[:, :half] = lo
        xb[:, half:] = hi

    @pl.when(jnp.logical_and(valid, j < n_up))
    def _():
        cols = pl.ds(pl.multiple_of(j * tf, tf), tf)
        for n in range(1, nsub_max + 1):
            @pl.when(nsub == n)
            def _(n=n):
                x = xb[0:n * sub, :]
                g = jnp.dot(x, wg_ref[...].astype(bf16), preferred_element_type=f32) + bg_ref[...]
                u = jnp.dot(x, wu_ref[...].astype(bf16), preferred_element_type=f32) + bu_ref[...]
                g = jnp.minimum(g, SWIGLU_LIMIT)
                u = jnp.clip(u, -SWIGLU_LIMIT, SWIGLU_LIMIT)
                a = g * jax.nn.sigmoid(SWIGLU_ALPHA * g) * (u + 1.0)
                act[0:n * sub, cols] = a.astype(bf16)

    @pl.when(jnp.logical_and(valid, j >= n_up))
    def _():
        for n in range(1, nsub_max + 1):
            @pl.when(nsub == n)
            def _(n=n):
                o_ref[0:n * sub, :] = jnp.dot(act[0:n * sub, :], wd_ref[...].astype(bf16),
                                              preferred_element_type=f32) + bd_ref[...]
                if n < nsub_max:
                    o_ref[n * sub:, :] = jnp.zeros((o_ref.shape[0] - n * sub, o_ref.shape[1]), f32)

    @pl.when(jnp.logical_and(jnp.logical_not(valid), j >= n_up))
    def _():
        o_ref[...] = jnp.zeros_like(o_ref)


def _moe(xs, tile_expert, tile_nsub, n_valid, w_gate, b_gate, w_up, b_up, w_down, b_down, layer, plan):
    n_rows, half = xs.shape
    d = 2 * half
    n_layer, n_exp, _, f = w_gate.shape
    tm, tf, td = plan.tm_moe, plan.tf_moe, plan.td_moe
    n_up, n_down = f // tf, d // td
    n_tile = n_rows // tm

    def tile(i, nv):
        return jnp.minimum(i, nv[0] - 1)

    def phase(i, j, nv):
        return jnp.where(i < nv[0], j, n_up + n_down - 1)

    def up_idx(i, j, te, ns, nv):
        return (layer, te[tile(i, nv)], 0, jnp.minimum(phase(i, j, nv), n_up - 1))

    def down_idx(i, j, te, ns, nv):
        return (layer, te[tile(i, nv)], 0, jnp.maximum(phase(i, j, nv) - n_up, 0))

    kern = functools.partial(_moe_kernel, n_up=n_up, tf=tf, sub=plan.sub_moe, nsub_max=plan.nsub_moe)
    return pl.pallas_call(
        kern,
        out_shape=jax.ShapeDtypeStruct((n_rows, d), f32),
        grid_spec=pltpu.PrefetchScalarGridSpec(
            num_scalar_prefetch=3,
            grid=(n_tile, n_up + n_down),
            in_specs=[
                pl.BlockSpec((tm, half), lambda i, j, te, ns, nv: (tile(i, nv), 0)),
                pl.BlockSpec((None, None, d, tf), up_idx),
                pl.BlockSpec((None, None, 1, tf), up_idx),
                pl.BlockSpec((None, None, d, tf), up_idx),
                pl.BlockSpec((None, None, 1, tf), up_idx),
                pl.BlockSpec((None, None, f, td), down_idx),
                pl.BlockSpec((None, None, 1, td), down_idx),
            ],
            out_specs=pl.BlockSpec((tm, td), lambda i, j, te, ns, nv: (i, jnp.maximum(j - n_up, 0))),
            scratch_shapes=[pltpu.VMEM((tm, d), bf16), pltpu.VMEM((tm, f), bf16)],
        ),
        compiler_params=_params("arbitrary", "arbitrary"),
        name="moe",
    )(tile_expert, tile_nsub, n_valid, xs, w_gate, b_gate.reshape(n_layer, n_exp, 1, f), w_up,
      b_up.reshape(n_layer, n_exp, 1, f), w_down, b_down.reshape(n_layer, n_exp, 1, d))


def _combine_kernel(pos_ref, nxt_ref, ys_ref, w_ref, x1_ref, gf_ref, g_ref, b_ref, o_ref, buf, sem,
                    *, tc, alpha):
    i = pl.program_id(0)
    n = pl.num_programs(0)
    slot = i % 2
    rows = TOP_K * tc

    def gather(p_ref, s):
        def issue(j, carry):
            for k in range(TOP_K):
                pltpu.make_async_copy(ys_ref.at[pl.ds(p_ref[0, k * tc + j], 1)],
                                      buf.at[s, pl.ds(k * tc + j, 1)], sem.at[s]).start(priority=k % 2)
            return carry
        lax.fori_loop(0, tc, issue, 0, unroll=ISSUE_UNROLL)

    @pl.when(i == 0)
    def _():
        gather(pos_ref, 0)

    for s in (0, 1):
        @pl.when(jnp.logical_and(i + 1 < n, slot != s))
        def _(s=s):
            gather(nxt_ref, s)

    for s in (0, 1):
        @pl.when(slot == s)
        def _(s=s):
            pltpu.make_async_copy(ys_ref.at[pl.ds(0, rows)], buf.at[s], sem.at[s]).wait()
            f = w_ref[:, 0:1] * buf[s, 0:tc, :]
            for k in range(1, TOP_K):
                f = f + w_ref[:, k:k + 1] * buf[s, k * tc:(k + 1) * tc, :]
            o_ref[...] = _layernorm(alpha * x1_ref[...] + gf_ref[...] * f) * g_ref[...] + b_ref[...]


def _combine(ys, pos_blocks, top_w, x1, gf, ln_g, ln_b, seq, alpha, plan):
    n_tok, d = x1.shape
    tc = plan.tc_comb
    n_step = n_tok // tc
    per_seq = seq // tc
    kern = functools.partial(_combine_kernel, tc=tc, alpha=alpha)
    smem = functools.partial(pl.BlockSpec, (None, 1, TOP_K * tc), memory_space=pltpu.SMEM)
    return pl.pallas_call(
        kern,
        out_shape=jax.ShapeDtypeStruct((n_tok, d), f32),
        grid=(n_step,),
        in_specs=[
            smem(lambda i: (i, 0, 0)),
            smem(lambda i: (jnp.minimum(i + 1, n_step - 1), 0, 0)),
            pl.BlockSpec(memory_space=pl.ANY),
            pl.BlockSpec((tc, TOP_K), lambda i: (i, 0)),
            pl.BlockSpec((tc, d), lambda i: (i, 0)),
            pl.BlockSpec((None, 1, d), lambda i: (i // per_seq, 0, 0)),
            pl.BlockSpec((1, d), lambda i: (0, 0)),
            pl.BlockSpec((1, d), lambda i: (0, 0)),
        ],
        out_specs=pl.BlockSpec((tc, d), lambda i: (i, 0)),
        scratch_shapes=[pltpu.VMEM((2, TOP_K * tc, d), f32), pltpu.SemaphoreType.DMA((2,))],
        compiler_params=_params("arbitrary"),
        name="combine",
    )(pos_blocks, pos_blocks, ys, top_w, x1, gf, ln_g.reshape(1, d), ln_b.reshape(1, d))


def _pos_blocks(pos, tc):
    k, n_tok = pos.shape
    return pos.reshape(k, n_tok // tc, tc).transpose(1, 0, 2).reshape(n_tok // tc, 1, k * tc)


def _tile_table(counts, tm, sub, n_tile):
    n_exp = counts.shape[0]
    tiles = (counts + tm - 1) // tm
    ends = jnp.cumsum(tiles)
    first = ends - tiles
    t = jnp.arange(n_tile, dtype=i32)
    tile_expert = jnp.minimum(jnp.sum(t[:, None] >= ends[None, :], axis=1), n_exp - 1).astype(i32)
    rows = jnp.clip(counts[tile_expert] - (t - first[tile_expert]) * tm, 0, tm)
    tile_nsub = ((rows + sub - 1) // sub).astype(i32)
    pad_start = (first * tm + counts).astype(i32)
    pad_end = (ends * tm).astype(i32)
    return tile_expert, tile_nsub, ends[-1:].astype(i32), pad_start, pad_end


def kernel(x, c, w_ada, b_ada, w_in, b_in, w_pool, b_pool, pool_scale, w_dw, b_dw, conv_ln_g, conv_ln_b,
           w_out, b_out, ln1_g, ln1_b, w_router, b_router, w_gate, b_gate, w_up, b_up, w_down, b_down,
           ln2_g, ln2_b):
    batch, seq, d = x.shape
    depth = w_ada.shape[0]
    n_exp, f = w_gate.shape[1], w_gate.shape[3]
    n_tok = batch * seq
    alpha = (2.0 * depth) ** 0.25
    plan = _plan(seq, d, n_tok, f)
    n_tile = n_tok * TOP_K // plan.tm_moe + n_exp
    n_rows = n_tile * plan.tm_moe

    mod = _ada(c, w_ada, b_ada, plan)
    xt = x.reshape(n_tok, d)
    for l in range(depth):
        sh_m, sc_m, g_m, sh_f, sc_f, g_f = [m.reshape(batch, 1, d) for m in jnp.split(mod[l], N_MOD, axis=-1)]
        row = lambda a: a[l].reshape(1, -1)
        lw = dict(
            w_pool=w_pool[l].astype(bf16), b_pool=row(b_pool), pool_scale=row(pool_scale),
            w_dw=w_dw[l], b_dw=row(b_dw), conv_ln_g=row(conv_ln_g), conv_ln_b=row(conv_ln_b),
            w_out=w_out[l].astype(bf16), b_out=row(b_out), ln1_g=row(ln1_g), ln1_b=row(ln1_b),
            w_router=w_router[l].astype(bf16), b_router=row(b_router))
        z = _inproj(xt, sh_m, sc_m, w_in[l].astype(bf16), b_in[l], seq, plan)
        x1, h2, logits = _mix(z, xt, g_m, sh_f, sc_f, lw, seq, alpha, plan)
        top_w, pos, counts = _route(logits.T, plan)
        tile_expert, tile_nsub, n_valid, pad_start, pad_end = _tile_table(
            counts[:, 0], plan.tm_moe, plan.sub_moe, n_tile)
        xs = _dispatch(h2, _pos_blocks(pos, plan.tc_disp), pad_start, pad_end, n_valid, n_rows, plan)
        ys = _moe(xs, tile_expert, tile_nsub, n_valid, w_gate, b_gate, w_up, b_up, w_down, b_down, l, plan)
        xt = _combine(ys, _pos_blocks(pos, plan.tc_comb), top_w.T, x1, g_f, ln2_g[l], ln2_b[l], seq, alpha, plan)
    return xt.reshape(batch, seq, d)
```

```python
import functools
from typing import NamedTuple

import jax
import jax.numpy as jnp
from jax import lax
from jax.experimental import pallas as pl
from jax.experimental.pallas import tpu as pltpu

POOL_WINDOWS = (2, 4, 8, 16)
CONV_KERNEL = 31
TOP_K = 4
SWIGLU_LIMIT = 7.0
SWIGLU_ALPHA = 1.702
LN_EPS = 1e-5
N_MOD = 6

HALO = 32
LANES = 128
SUBLANES = 8
ISSUE_UNROLL = 4
VMEM_LIMIT = 58 * 1024 * 1024

f32 = jnp.float32
bf16 = jnp.bfloat16
i32 = jnp.int32
u32 = jnp.uint32


class Plan(NamedTuple):
    tn_ada: int
    tm_in: int
    tm_mix: int
    conv_rows: int
    tb_route: int
    sub_moe: int
    nsub_moe: int
    tf_moe: int
    td_moe: int
    tc_disp: int
    tc_comb: int

    @property
    def tm_moe(self):
        return self.sub_moe * self.nsub_moe


def _plan(seq, d_model, n_tok, d_expert):
    def fit(pref, n):
        t = min(pref, n)
        assert n % t == 0, (pref, n)
        return t
    return Plan(
        tn_ada=fit(1024, N_MOD * d_model),
        tm_in=fit(512, seq),
        tm_mix=fit(256, seq),
        conv_rows=64,
        tb_route=fit(512, n_tok),
        sub_moe=256,
        nsub_moe=5,
        tf_moe=fit(512, d_expert),
        td_moe=fit(512, d_model),
        tc_disp=fit(1024, n_tok),
        tc_comb=fit(128, n_tok),
    )


def _layernorm(x):
    mu = jnp.mean(x, axis=-1, keepdims=True)
    xc = x - mu
    var = jnp.mean(xc * xc, axis=-1, keepdims=True)
    return xc * lax.rsqrt(var + LN_EPS)


def _silu(x):
    return x * jax.nn.sigmoid(x)


def _pack_halves(xb):
    half = xb.shape[1] // 2
    lo = lax.bitcast_convert_type(xb[:, :half].astype(f32), u32) >> 16
    hi = lax.bitcast_convert_type(xb[:, half:].astype(f32), u32) & jnp.uint32(0xFFFF0000)
    return lo | hi


def _unpack_halves(w):
    lo = lax.bitcast_convert_type(w << 16, f32).astype(bf16)
    hi = lax.bitcast_convert_type(w & jnp.uint32(0xFFFF0000), f32).astype(bf16)
    return lo, hi


def _params(*sem):
    return pltpu.CompilerParams(dimension_semantics=sem, vmem_limit_bytes=VMEM_LIMIT)


def _ada_kernel(c_ref, w_ref, b_ref, o_ref):
    c = c_ref[...]
    o_ref[...] = jnp.dot(_silu(c).astype(bf16), w_ref[...].astype(bf16),
                         preferred_element_type=f32) + b_ref[...]


def _ada(c, w_ada, b_ada, plan):
    n_layer, d, n = w_ada.shape
    b = c.shape[0]
    tn = plan.tn_ada
    return pl.pallas_call(
        _ada_kernel,
        out_shape=jax.ShapeDtypeStruct((n_layer, b, n), f32),
        grid=(n_layer, n // tn),
        in_specs=[
            pl.BlockSpec((b, d), lambda l, j: (0, 0)),
            pl.BlockSpec((None, d, tn), lambda l, j: (l, 0, j)),
            pl.BlockSpec((None, 1, tn), lambda l, j: (l, 0, j)),
        ],
        out_specs=pl.BlockSpec((None, b, tn), lambda l, j: (l, 0, j)),
        compiler_params=_params("arbitrary", "arbitrary"),
        name="ada",
    )(c, w_ada, b_ada.reshape(n_layer, 1, n))


def _inproj_kernel(x_ref, sh_ref, sc_ref, w_ref, b_ref, z_ref):
    h = _layernorm(x_ref[...]) * (1.0 + sc_ref[...]) + sh_ref[...]
    z_ref[...] = jnp.dot(h.astype(bf16), w_ref[...], preferred_element_type=f32) + b_ref[...]


def _inproj(x, sh, sc, w_in, b_in, seq, plan):
    n_tok, d = x.shape
    n = w_in.shape[1]
    tm = plan.tm_in
    per_seq = seq // tm
    vec = pl.BlockSpec((None, 1, d), lambda i: (i // per_seq, 0, 0))
    return pl.pallas_call(
        _inproj_kernel,
        out_shape=jax.ShapeDtypeStruct((n_tok, n), f32),
        grid=(n_tok // tm,),
        in_specs=[
            pl.BlockSpec((tm, d), lambda i: (i, 0)),
            vec, vec,
            pl.BlockSpec((d, n), lambda i: (0, 0)),
            pl.BlockSpec((1, n), lambda i: (0, 0)),
        ],
        out_specs=pl.BlockSpec((tm, n), lambda i: (i, 0)),
        compiler_params=_params("arbitrary"),
        name="inproj",
    )(x, sh, sc, w_in, b_in.reshape(1, n))


def _mix_kernel(zc_ref, zp_ref, x_ref, gm_ref, shf_ref, scf_ref,
                wpool_ref, bpool_ref, pscale_ref, wdw_ref, bdw_ref, cg_ref, cb_ref,
                wout_ref, bout_ref, l1g_ref, l1b_ref, wr_ref, br_ref,
                x1_ref, h2_ref, lg_ref,
                zbuf, ubuf, cbuf, ycat, *, per_seq, alpha, conv_rows):
    tm, d = x_ref.shape
    pw = d // 2
    cw = d - pw
    grp = pw // len(POOL_WINDOWS)
    i = pl.program_id(0)
    tile_in_seq = i % per_seq
    first = tile_in_seq == 0

    zbuf[0:HALO, :] = jnp.where(first, 0.0, zp_ref[:, 0:pw])
    zbuf[HALO:, :] = zc_ref[:, 0:pw]
    ubuf[0:HALO, :] = jnp.where(first, 0.0, zp_ref[:, pw:pw + cw] * jax.nn.sigmoid(zp_ref[:, pw + cw:]))
    ubuf[HALO:, :] = zc_ref[:, pw:pw + cw] * jax.nn.sigmoid(zc_ref[:, pw + cw:])

    t_pos = (tile_in_seq * tm + lax.broadcasted_iota(i32, (tm, 1), 0) + 1).astype(f32)
    for g, w in enumerate(POOL_WINDOWS):
        cols = slice(g * grp, (g + 1) * grp)
        tok = zbuf[HALO:HALO + tm, cols]
        win = tok
        for k in range(1, w):
            win = win + zbuf[HALO - k:HALO - k + tm, cols]
        pooled = win / jnp.minimum(t_pos, float(w)) - tok
        ya = jnp.dot(pooled.astype(bf16), wpool_ref[g], preferred_element_type=f32) + bpool_ref[:, cols]
        ycat[:, cols] = (ya * pscale_ref[:, cols]).astype(bf16)

    lead = HALO - (CONV_KERNEL - 1)

    def conv_lanes(c, carry):
        lanes = pl.ds(pl.multiple_of(c * LANES, LANES), LANES)
        for r0 in range(0, tm, conv_rows):
            acc = bdw_ref[:, lanes]
            for shift in range(SUBLANES):
                taps = [k for k in range(CONV_KERNEL) if (lead + k) % SUBLANES == shift]
                if not taps:
                    continue
                span = conv_rows + (SUBLANES if shift else 0)
                part = None
                for k in taps:
                    row = r0 + (lead + k) - shift
                    term = wdw_ref[k:k + 1, lanes] * ubuf[row:row + span, lanes]
                    part = term if part is None else part + term
                acc = acc + part[shift:shift + conv_rows, :]
            cbuf[r0:r0 + conv_rows, lanes] = acc
        return carry
    lax.fori_loop(0, cw // LANES, conv_lanes, 0)
    yb = _silu(_layernorm(cbuf[...]) * cg_ref[...] + cb_ref[...])
    ycat[:, pw:] = yb.astype(bf16)

    y = jnp.dot(ycat[...], wout_ref[...], preferred_element_type=f32) + bout_ref[...]
    x1 = _layernorm(alpha * x_ref[...] + gm_ref[...] * y) * l1g_ref[...] + l1b_ref[...]
    x1_ref[...] = x1
    h2 = (_layernorm(x1) * (1.0 + scf_ref[...]) + shf_ref[...]).astype(bf16)
    h2_ref[...] = _pack_halves(h2)
    lg_ref[...] = jnp.dot(h2, wr_ref[...], preferred_element_type=f32) + br_ref[...]


def _mix(z, x, gm, shf, scf, lw, seq, alpha, plan):
    n_tok, d = x.shape
    nz = z.shape[1]
    n_exp = lw["w_router"].shape[1]
    pw = d // 2
    cw = d - pw
    tm = plan.tm_mix
    per_seq = seq // tm
    halo_per_tile = tm // HALO
    vec = pl.BlockSpec((None, 1, d), lambda i: (i // per_seq, 0, 0))

    def whole(a):
        return pl.BlockSpec(a.shape, lambda i: (0,) * a.ndim)

    weights = [lw["w_pool"], lw["b_pool"], lw["pool_scale"], lw["w_dw"], lw["b_dw"], lw["conv_ln_g"],
               lw["conv_ln_b"], lw["w_out"], lw["b_out"], lw["ln1_g"], lw["ln1_b"], lw["w_router"], lw["b_router"]]
    kern = functools.partial(_mix_kernel, per_seq=per_seq, alpha=alpha, conv_rows=min(plan.conv_rows, tm))
    return pl.pallas_call(
        kern,
        out_shape=(jax.ShapeDtypeStruct((n_tok, d), f32),
                   jax.ShapeDtypeStruct((n_tok, d // 2), u32),
                   jax.ShapeDtypeStruct((n_tok, n_exp), f32)),
        grid=(n_tok // tm,),
        in_specs=[
            pl.BlockSpec((tm, nz), lambda i: (i, 0)),
            pl.BlockSpec((HALO, nz), lambda i: (jnp.maximum(i * halo_per_tile - 1, 0), 0)),
            pl.BlockSpec((tm, d), lambda i: (i, 0)),
            vec, vec, vec,
        ] + [whole(a) for a in weights],
        out_specs=(pl.BlockSpec((tm, d), lambda i: (i, 0)),
                   pl.BlockSpec((tm, d // 2), lambda i: (i, 0)),
                   pl.BlockSpec((tm, n_exp), lambda i: (i, 0))),
        scratch_shapes=[pltpu.VMEM((tm + HALO, pw), f32), pltpu.VMEM((tm + HALO, cw), f32),
                        pltpu.VMEM((tm, cw), f32), pltpu.VMEM((tm, d), bf16)],
        compiler_params=_params("arbitrary"),
        name="mix",
    )(z, z, x, gm, shf, scf, *weights)


def _route_kernel(lg_ref, w_ref, pos_ref, cnt_ref, idx_s, rank_s, *, tb, granule):
    n_exp, n_tok = lg_ref.shape
    iota_e = lax.broadcasted_iota(i32, (n_exp, tb), 0)
    before = (lax.broadcasted_iota(i32, (tb, tb), 0) < lax.broadcasted_iota(i32, (tb, tb), 1)).astype(bf16)

    def select(b, seen):
        blk = pl.ds(pl.multiple_of(b * tb, tb), tb)
        work = lg_ref[:, blk]
        vals, sels = [], []
        for k in range(TOP_K):
            m = jnp.max(work, axis=0, keepdims=True)
            idx = jnp.min(jnp.where(work == m, iota_e, n_exp), axis=0, keepdims=True)
            sel = iota_e == idx
            vals.append(m)
            sels.append(sel)
            idx_s[k:k + 1, blk] = idx
            work = jnp.where(sel, -jnp.inf, work)
        exps = [jnp.exp(v - vals[0]) for v in vals]
        den = exps[0]
        for e in exps[1:]:
            den = den + e
        chosen = jnp.zeros((n_exp, tb), f32)
        for k in range(TOP_K):
            w_ref[k:k + 1, blk] = exps[k] / den
            chosen = chosen + sels[k].astype(f32)
        earlier = jnp.dot(chosen.astype(bf16), before, preferred_element_type=f32) + seen
        for k in range(TOP_K):
            rank_s[k:k + 1, blk] = jnp.sum(jnp.where(sels[k], earlier, 0.0), axis=0, keepdims=True)
        return seen + jnp.sum(chosen, axis=1, keepdims=True)

    counts = lax.fori_loop(0, n_tok // tb, select, jnp.zeros((n_exp, 1), f32))

    padded = jnp.ceil(counts / granule) * granule
    ee_r = lax.broadcasted_iota(i32, (n_exp, n_exp), 0)
    ee_c = lax.broadcasted_iota(i32, (n_exp, n_exp), 1)
    padded_row = jnp.sum(jnp.where(ee_r == ee_c, padded, 0.0), axis=0, keepdims=True)
    starts = jnp.sum(jnp.where(ee_c < ee_r, padded_row, 0.0), axis=1, keepdims=True)

    def place(b, carry):
        blk = pl.ds(pl.multiple_of(b * tb, tb), tb)
        for k in range(TOP_K):
            sel = iota_e == idx_s[k:k + 1, blk]
            start = jnp.sum(jnp.where(sel, starts, 0.0), axis=0, keepdims=True)
            pos_ref[k:k + 1, blk] = (start + rank_s[k:k + 1, blk]).astype(i32)
        return carry

    lax.fori_loop(0, n_tok // tb, place, 0)
    cnt_ref[...] = jnp.broadcast_to(counts, cnt_ref.shape).astype(i32)


def _route(logits_t, plan):
    n_exp, n_tok = logits_t.shape
    kern = functools.partial(_route_kernel, tb=plan.tb_route, granule=float(plan.tm_moe))
    return pl.pallas_call(
        kern,
        out_shape=(jax.ShapeDtypeStruct((TOP_K, n_tok), f32),
                   jax.ShapeDtypeStruct((TOP_K, n_tok), i32),
                   jax.ShapeDtypeStruct((n_exp, LANES), i32)),
        scratch_shapes=[pltpu.VMEM((TOP_K, n_tok), i32), pltpu.VMEM((TOP_K, n_tok), f32)],
        compiler_params=pltpu.CompilerParams(vmem_limit_bytes=VMEM_LIMIT),
        name="route",
    )(logits_t)


def _dispatch_kernel(pstart_ref, pend_ref, nv_ref, pos_ref, h_ref, xs_ref, zrows, sem, zsem, *, tc, tm):
    c = pl.program_id(0)
    n_exp = pstart_ref.shape[0]
    zr = zrows.shape[0]

    def issue(j, carry):
        for k in range(TOP_K):
            pltpu.make_async_copy(h_ref.at[pl.ds(j, 1)], xs_ref.at[pl.ds(pos_ref[0, k * tc + j], 1)],
                                  sem).start(priority=k % 2)
        return carry
    lax.fori_loop(0, tc, issue, 0)

    step_rows = xs_ref.at[pl.ds(0, TOP_K * tc)]
    pltpu.make_async_copy(step_rows, step_rows, sem).wait()

    @pl.when(c == pl.num_programs(0) - 1)
    def _():
        zrows[...] = jnp.zeros_like(zrows)

        def row_fill(p):
            return pltpu.make_async_copy(zrows.at[pl.ds(0, 1)], xs_ref.at[pl.ds(p, 1)], zsem)

        def chunk_fill(q):
            return pltpu.make_async_copy(zrows, xs_ref.at[pl.ds(pl.multiple_of(q * zr, zr), zr)], zsem)

        def fill(lo, hi):
            mid = jnp.minimum((lo + (zr - 1)) // zr * zr, hi)
            lax.fori_loop(lo, mid, lambda p, a: (row_fill(p).start(), a)[1], 0)
            lax.fori_loop(lo, mid, lambda p, a: (row_fill(0).wait(), a)[1], 0)
            lax.fori_loop(mid // zr, hi // zr, lambda q, a: (chunk_fill(q).start(), a)[1], 0)
            lax.fori_loop(mid // zr, hi // zr, lambda q, a: (chunk_fill(0).wait(), a)[1], 0)

        def per_expert(e, carry):
            fill(pstart_ref[e], pend_ref[e])
            return carry
        lax.fori_loop(0, n_exp, per_expert, 0)
        fill(nv_ref[0] * tm, xs_ref.shape[0])


def _dispatch(h2, pos_blocks, pad_start, pad_end, n_valid, n_rows, plan):
    n_tok, dw = h2.shape
    tc = plan.tc_disp
    zr = min(plan.sub_moe, 128)
    assert plan.tm_moe % zr == 0
    kern = functools.partial(_dispatch_kernel, tc=tc, tm=plan.tm_moe)
    return pl.pallas_call(
        kern,
        out_shape=jax.ShapeDtypeStruct((n_rows, dw), h2.dtype),
        grid_spec=pltpu.PrefetchScalarGridSpec(
            num_scalar_prefetch=3,
            grid=(n_tok // tc,),
            in_specs=[
                pl.BlockSpec((None, 1, TOP_K * tc), lambda c, *_: (c, 0, 0), memory_space=pltpu.SMEM),
                pl.BlockSpec((tc, dw), lambda c, *_: (c, 0)),
            ],
            out_specs=pl.BlockSpec(memory_space=pl.ANY),
            scratch_shapes=[pltpu.VMEM((zr, dw), h2.dtype), pltpu.SemaphoreType.DMA, pltpu.SemaphoreType.DMA],
        ),
        compiler_params=_params("arbitrary"),
        name="dispatch",
    )(pad_start, pad_end, n_valid, pos_blocks, h2)


def _moe_kernel(te_ref, ns_ref, nv_ref, x_ref, wg_ref, bg_ref, wu_ref, bu_ref, wd_ref, bd_ref, o_ref,
                xb, act, wdbuf, wdsem, *, layer, n_up, n_down, tf, sub, nsub_max):
    i = pl.program_id(0)
    j = pl.program_id(1)
    valid = i < nv_ref[0]
    tile = jnp.minimum(i, nv_ref[0] - 1)
    nsub = ns_ref[tile]
    expert = te_ref[tile]
    half = x_ref.shape[1]
    td = wdbuf.shape[2]

    def wd_copy(c, slot):
        return pltpu.make_async_copy(wd_ref.at[layer, expert, :, pl.ds(pl.multiple_of(c * td, td), td)],
                                     wdbuf.at[slot], wdsem.at[slot])

    @pl.when(jnp.logical_and(valid, j == 0))
    def _():
        for c in range(min(2, n_down)):
            wd_copy(c, c).start()
        lo, hi = _unpack_halves(x_ref[...])
        xb[:, :half] = lo
        xb[:, half:] = hi

    @pl.when(jnp.logical_and(valid, j < n_up))
    def _():
        cols = pl.ds(pl.multiple_of(j * tf, tf), tf)
        for n in range(1, nsub_max + 1):
            @pl.when(nsub == n)
            def _(n=n):
                x = xb[0:n * sub, :]
                g = jnp.dot(x, wg_ref[...].astype(bf16), preferred_element_type=f32) + bg_ref[...]
                u = jnp.dot(x, wu_ref[...].astype(bf16), preferred_element_type=f32) + bu_ref[...]
                g = jnp.minimum(g, SWIGLU_LIMIT)
                u = jnp.clip(u, -SWIGLU_LIMIT, SWIGLU_LIMIT)
                a = g * jax.nn.sigmoid(SWIGLU_ALPHA * g) * (u + 1.0)
                act[0:n * sub, cols] = a.astype(bf16)

    @pl.when(jnp.logical_and(valid, j >= n_up))
    def _():
        c = j - n_up
        slot = c % 2
        wd_copy(c, slot).wait()
        for n in range(1, nsub_max + 1):
            @pl.when(nsub == n)
            def _(n=n):
                o_ref[0:n * sub, :] = jnp.dot(act[0:n * sub, :], wdbuf[slot].astype(bf16),
                                              preferred_element_type=f32) + bd_ref[...]
                if n < nsub_max:
                    o_ref[n * sub:, :] = jnp.zeros((o_ref.shape[0] - n * sub, o_ref.shape[1]), f32)

        @pl.when(c + 2 < n_down)
        def _():
            wd_copy(c + 2, slot).start()

    @pl.when(jnp.logical_and(jnp.logical_not(valid), j >= n_up))
    def _():
        o_ref[...] = jnp.zeros_like(o_ref)


def _moe(xs, tile_expert, tile_nsub, n_valid, w_gate, b_gate, w_up, b_up, w_down, b_down, layer, plan):
    n_rows, half = xs.shape
    d = 2 * half
    n_layer, n_exp, _, f = w_gate.shape
    tm, tf, td = plan.tm_moe, plan.tf_moe, plan.td_moe
    n_up, n_down = f // tf, d // td
    n_tile = n_rows // tm

    def tile(i, nv):
        return jnp.minimum(i, nv[0] - 1)

    def phase(i, j, nv):
        return jnp.where(i < nv[0], j, n_up + n_down - 1)

    def up_idx(i, j, te, ns, nv):
        return (layer, te[tile(i, nv)], 0, jnp.minimum(phase(i, j, nv), n_up - 1))

    def down_idx(i, j, te, ns, nv):
        return (layer, te[tile(i, nv)], 0, jnp.maximum(phase(i, j, nv) - n_up, 0))

    kern = functools.partial(_moe_kernel, layer=layer, n_up=n_up, n_down=n_down, tf=tf, sub=plan.sub_moe,
                             nsub_max=plan.nsub_moe)
    return pl.pallas_call(
        kern,
        out_shape=jax.ShapeDtypeStruct((n_rows, d), f32),
        grid_spec=pltpu.PrefetchScalarGridSpec(
            num_scalar_prefetch=3,
            grid=(n_tile, n_up + n_down),
            in_specs=[
                pl.BlockSpec((tm, half), lambda i, j, te, ns, nv: (tile(i, nv), 0)),
                pl.BlockSpec((None, None, d, tf), up_idx),
                pl.BlockSpec((None, None, 1, tf), up_idx),
                pl.BlockSpec((None, None, d, tf), up_idx),
                pl.BlockSpec((None, None, 1, tf), up_idx),
                pl.BlockSpec(memory_space=pl.ANY),
                pl.BlockSpec((None, None, 1, td), down_idx),
            ],
            out_specs=pl.BlockSpec((tm, td), lambda i, j, te, ns, nv: (i, jnp.maximum(j - n_up, 0))),
            scratch_shapes=[pltpu.VMEM((tm, d), bf16), pltpu.VMEM((tm, f), bf16),
                            pltpu.VMEM((2, f, td), f32), pltpu.SemaphoreType.DMA((2,))],
        ),
        compiler_params=_params("arbitrary", "arbitrary"),
        name="moe",
    )(tile_expert, tile_nsub, n_valid, xs, w_gate, b_gate.reshape(n_layer, n_exp, 1, f), w_up,
      b_up.reshape(n_layer, n_exp, 1, f), w_down, b_down.reshape(n_layer, n_exp, 1, d))


def _combine_kernel(pos_ref, nxt_ref, ys_ref, w_ref, x1_ref, gf_ref, g_ref, b_ref, o_ref, buf, sem,
                    *, tc, alpha):
    i = pl.program_id(0)
    n = pl.num_programs(0)
    slot = i % 2
    rows = TOP_K * tc

    def gather(p_ref, s):
        def issue(j, carry):
            for k in range(TOP_K):
                pltpu.make_async_copy(ys_ref.at[pl.ds(p_ref[0, k * tc + j], 1)],
                                      buf.at[s, pl.ds(k * tc + j, 1)], sem.at[s]).start(priority=k % 2)
            return carry
        lax.fori_loop(0, tc, issue, 0, unroll=ISSUE_UNROLL)

    @pl.when(i == 0)
    def _():
        gather(pos_ref, 0)

    for s in (0, 1):
        @pl.when(jnp.logical_and(i + 1 < n, slot != s))
        def _(s=s):
            gather(nxt_ref, s)

    for s in (0, 1):
        @pl.when(slot == s)
        def _(s=s):
            pltpu.make_async_copy(ys_ref.at[pl.ds(0, rows)], buf.at[s], sem.at[s]).wait()
            f = w_ref[:, 0:1] * buf[s, 0:tc, :]
            for k in range(1, TOP_K):
                f = f + w_ref[:, k:k + 1] * buf[s, k * tc:(k + 1) * tc, :]
            o_ref[...] = _layernorm(alpha * x1_ref[...] + gf_ref[...] * f) * g_ref[...] + b_ref[...]


def _combine(ys, pos_blocks, top_w, x1, gf, ln_g, ln_b, seq, alpha, plan):
    n_tok, d = x1.shape
    tc = plan.tc_comb
    n_step = n_tok // tc
    per_seq = seq // tc
    kern = functools.partial(_combine_kernel, tc=tc, alpha=alpha)
    smem = functools.partial(pl.BlockSpec, (None, 1, TOP_K * tc), memory_space=pltpu.SMEM)
    return pl.pallas_call(
        kern,
        out_shape=jax.ShapeDtypeStruct((n_tok, d), f32),
        grid=(n_step,),
        in_specs=[
            smem(lambda i: (i, 0, 0)),
            smem(lambda i: (jnp.minimum(i + 1, n_step - 1), 0, 0)),
            pl.BlockSpec(memory_space=pl.ANY),
            pl.BlockSpec((tc, TOP_K), lambda i: (i, 0)),
            pl.BlockSpec((tc, d), lambda i: (i, 0)),
            pl.BlockSpec((None, 1, d), lambda i: (i // per_seq, 0, 0)),
            pl.BlockSpec((1, d), lambda i: (0, 0)),
            pl.BlockSpec((1, d), lambda i: (0, 0)),
        ],
        out_specs=pl.BlockSpec((tc, d), lambda i: (i, 0)),
        scratch_shapes=[pltpu.VMEM((2, TOP_K * tc, d), f32), pltpu.SemaphoreType.DMA((2,))],
        compiler_params=_params("arbitrary"),
        name="combine",
    )(pos_blocks, pos_blocks, ys, top_w, x1, gf, ln_g.reshape(1, d), ln_b.reshape(1, d))


def _pos_blocks(pos, tc):
    k, n_tok = pos.shape
    return pos.reshape(k, n_tok // tc, tc).transpose(1, 0, 2).reshape(n_tok // tc, 1, k * tc)


def _tile_table(counts, tm, sub, n_tile):
    n_exp = counts.shape[0]
    tiles = (counts + tm - 1) // tm
    ends = jnp.cumsum(tiles)
    first = ends - tiles
    t = jnp.arange(n_tile, dtype=i32)
    tile_expert = jnp.minimum(jnp.sum(t[:, None] >= ends[None, :], axis=1), n_exp - 1).astype(i32)
    rows = jnp.clip(counts[tile_expert] - (t - first[tile_expert]) * tm, 0, tm)
    tile_nsub = ((rows + sub - 1) // sub).astype(i32)
    pad_start = (first * tm + counts).astype(i32)
    pad_end = (ends * tm).astype(i32)
    return tile_expert, tile_nsub, ends[-1:].astype(i32), pad_start, pad_end


def kernel(x, c, w_ada, b_ada, w_in, b_in, w_pool, b_pool, pool_scale, w_dw, b_dw, conv_ln_g, conv_ln_b,
           w_out, b_out, ln1_g, ln1_b, w_router, b_router, w_gate, b_gate, w_up, b_up, w_down, b_down,
           ln2_g, ln2_b):
    batch, seq, d = x.shape
    depth = w_ada.shape[0]
    n_exp, f = w_gate.shape[1], w_gate.shape[3]
    n_tok = batch * seq
    alpha = (2.0 * depth) ** 0.25
    plan = _plan(seq, d, n_tok, f)
    n_tile = n_tok * TOP_K // plan.tm_moe + n_exp
    n_rows = n_tile * plan.tm_moe

    mod = _ada(c, w_ada, b_ada, plan)
    xt = x.reshape(n_tok, d)
    for l in range(depth):
        sh_m, sc_m, g_m, sh_f, sc_f, g_f = [m.reshape(batch, 1, d) for m in jnp.split(mod[l], N_MOD, axis=-1)]
        row = lambda a: a[l].reshape(1, -1)
        lw = dict(
            w_pool=w_pool[l].astype(bf16), b_pool=row(b_pool), pool_scale=row(pool_scale),
            w_dw=w_dw[l], b_dw=row(b_dw), conv_ln_g=row(conv_ln_g), conv_ln_b=row(conv_ln_b),
            w_out=w_out[l].astype(bf16), b_out=row(b_out), ln1_g=row(ln1_g), ln1_b=row(ln1_b),
            w_router=w_router[l].astype(bf16), b_router=row(b_router))
        z = _inproj(xt, sh_m, sc_m, w_in[l].astype(bf16), b_in[l], seq, plan)
        x1, h2, logits = _mix(z, xt, g_m, sh_f, sc_f, lw, seq, alpha, plan)
        top_w, pos, counts = _route(logits.T, plan)
        tile_expert, tile_nsub, n_valid, pad_start, pad_end = _tile_table(
            counts[:, 0], plan.tm_moe, plan.sub_moe, n_tile)
        xs = _dispatch(h2, _pos_blocks(pos, plan.tc_disp), pad_start, pad_end, n_valid, n_rows, plan)
        ys = _moe(xs, tile_expert, tile_nsub, n_valid, w_gate, b_gate, w_up, b_up, w_down, b_down, l, plan)
        xt = _combine(ys, _pos_blocks(pos, plan.tc_comb), top_w.T, x1, g_f, ln2_g[l], ln2_b[l], seq, alpha, plan)
    return xt.reshape(batch, seq, d)
```

```python
import functools
from typing import NamedTuple

import jax
import jax.numpy as jnp
from jax import lax
from jax.experimental import pallas as pl
from jax.experimental.pallas import tpu as pltpu

POOL_WINDOWS = (2, 4, 8, 16)
CONV_KERNEL = 31
TOP_K = 4
SWIGLU_LIMIT = 7.0
SWIGLU_ALPHA = 1.702
LN_EPS = 1e-5
N_MOD = 6

HALO = 32
LANES = 128
SUBLANES = 8
ISSUE_UNROLL = 4
VMEM_LIMIT = 58 * 1024 * 1024

f32 = jnp.float32
bf16 = jnp.bfloat16
i32 = jnp.int32
u32 = jnp.uint32


class Plan(NamedTuple):
    tn_ada: int
    tm_in: int
    tm_mix: int
    conv_rows: int
    tb_route: int
    sub_moe: int
    nsub_moe: int
    tf_moe: int
    td_moe: int
    tc_disp: int
    tc_comb: int

    @property
    def tm_moe(self):
        return self.sub_moe * self.nsub_moe


def _plan(seq, d_model, n_tok, d_expert):
    def fit(pref, n):
        t = min(pref, n)
        assert n % t == 0, (pref, n)
        return t
    return Plan(
        tn_ada=fit(1024, N_MOD * d_model),
        tm_in=fit(512, seq),
        tm_mix=fit(256, seq),
        conv_rows=64,
        tb_route=fit(512, n_tok),
        sub_moe=256,
        nsub_moe=5,
        tf_moe=fit(512, d_expert),
        td_moe=fit(512, d_model),
        tc_disp=fit(1024, n_tok),
        tc_comb=fit(256, n_tok),
    )


def _layernorm(x):
    mu = jnp.mean(x, axis=-1, keepdims=True)
    xc = x - mu
    var = jnp.mean(xc * xc, axis=-1, keepdims=True)
    return xc * lax.rsqrt(var + LN_EPS)


def _silu(x):
    return x * jax.nn.sigmoid(x)


def _pack_halves(xb):
    half = xb.shape[1] // 2
    lo = lax.bitcast_convert_type(xb[:, :half].astype(f32), u32) >> 16
    hi = lax.bitcast_convert_type(xb[:, half:].astype(f32), u32) & jnp.uint32(0xFFFF0000)
    return lo | hi


def _unpack_halves(w):
    lo = lax.bitcast_convert_type(w << 16, f32).astype(bf16)
    hi = lax.bitcast_convert_type(w & jnp.uint32(0xFFFF0000), f32).astype(bf16)
    return lo, hi


def _params(*sem):
    return pltpu.CompilerParams(dimension_semantics=sem, vmem_limit_bytes=VMEM_LIMIT)


def _ada_kernel(c_ref, w_ref, b_ref, o_ref):
    c = c_ref[...]
    o_ref[...] = jnp.dot(_silu(c).astype(bf16), w_ref[...].astype(bf16),
                         preferred_element_type=f32) + b_ref[...]


def _ada(c, w_ada, b_ada, plan):
    n_layer, d, n = w_ada.shape
    b = c.shape[0]
    tn = plan.tn_ada
    return pl.pallas_call(
        _ada_kernel,
        out_shape=jax.ShapeDtypeStruct((n_layer, b, n), f32),
        grid=(n_layer, n // tn),
        in_specs=[
            pl.BlockSpec((b, d), lambda l, j: (0, 0)),
            pl.BlockSpec((None, d, tn), lambda l, j: (l, 0, j)),
            pl.BlockSpec((None, 1, tn), lambda l, j: (l, 0, j)),
        ],
        out_specs=pl.BlockSpec((None, b, tn), lambda l, j: (l, 0, j)),
        compiler_params=_params("arbitrary", "arbitrary"),
        name="ada",
    )(c, w_ada, b_ada.reshape(n_layer, 1, n))


def _inproj_kernel(x_ref, sh_ref, sc_ref, w_ref, b_ref, z_ref):
    h = _layernorm(x_ref[...]) * (1.0 + sc_ref[...]) + sh_ref[...]
    z_ref[...] = jnp.dot(h.astype(bf16), w_ref[...], preferred_element_type=f32) + b_ref[...]


def _inproj(x, sh, sc, w_in, b_in, seq, plan):
    n_tok, d = x.shape
    n = w_in.shape[1]
    tm = plan.tm_in
    per_seq = seq // tm
    vec = pl.BlockSpec((None, 1, d), lambda i: (i // per_seq, 0, 0))
    return pl.pallas_call(
        _inproj_kernel,
        out_shape=jax.ShapeDtypeStruct((n_tok, n), f32),
        grid=(n_tok // tm,),
        in_specs=[
            pl.BlockSpec((tm, d), lambda i: (i, 0)),
            vec, vec,
            pl.BlockSpec((d, n), lambda i: (0, 0)),
            pl.BlockSpec((1, n), lambda i: (0, 0)),
        ],
        out_specs=pl.BlockSpec((tm, n), lambda i: (i, 0)),
        compiler_params=_params("arbitrary"),
        name="inproj",
    )(x, sh, sc, w_in, b_in.reshape(1, n))


def _mix_kernel(zc_ref, zp_ref, x_ref, gm_ref, shf_ref, scf_ref,
                wpool_ref, bpool_ref, pscale_ref, wdw_ref, bdw_ref, cg_ref, cb_ref,
                wout_ref, bout_ref, l1g_ref, l1b_ref, wr_ref, br_ref,
                x1_ref, h2_ref, lg_ref,
                zbuf, ubuf, cbuf, ycat, *, per_seq, alpha, conv_rows):
    tm, d = x_ref.shape
    pw = d // 2
    cw = d - pw
    grp = pw // len(POOL_WINDOWS)
    i = pl.program_id(0)
    tile_in_seq = i % per_seq
    first = tile_in_seq == 0

    zbuf[0:HALO, :] = jnp.where(first, 0.0, zp_ref[:, 0:pw])
    zbuf[HALO:, :] = zc_ref[:, 0:pw]
    ubuf[0:HALO, :] = jnp.where(first, 0.0, zp_ref[:, pw:pw + cw] * jax.nn.sigmoid(zp_ref[:, pw + cw:]))
    ubuf[HALO:, :] = zc_ref[:, pw:pw + cw] * jax.nn.sigmoid(zc_ref[:, pw + cw:])

    t_pos = (tile_in_seq * tm + lax.broadcasted_iota(i32, (tm, 1), 0) + 1).astype(f32)
    for g, w in enumerate(POOL_WINDOWS):
        cols = slice(g * grp, (g + 1) * grp)
        tok = zbuf[HALO:HALO + tm, cols]
        win = tok
        for k in range(1, w):
            win = win + zbuf[HALO - k:HALO - k + tm, cols]
        pooled = win / jnp.minimum(t_pos, float(w)) - tok
        ya = jnp.dot(pooled.astype(bf16), wpool_ref[g], preferred_element_type=f32) + bpool_ref[:, cols]
        ycat[:, cols] = (ya * pscale_ref[:, cols]).astype(bf16)

    lead = HALO - (CONV_KERNEL - 1)

    def conv_lanes(c, carry):
        lanes = pl.ds(pl.multiple_of(c * LANES, LANES), LANES)
        for r0 in range(0, tm, conv_rows):
            acc = bdw_ref[:, lanes]
            for shift in range(SUBLANES):
                taps = [k for k in range(CONV_KERNEL) if (lead + k) % SUBLANES == shift]
                if not taps:
                    continue
                span = conv_rows + (SUBLANES if shift else 0)
                part = None
                for k in taps:
                    row = r0 + (lead + k) - shift
                    term = wdw_ref[k:k + 1, lanes] * ubuf[row:row + span, lanes]
                    part = term if part is None else part + term
                acc = acc + part[shift:shift + conv_rows, :]
            cbuf[r0:r0 + conv_rows, lanes] = acc
        return carry
    lax.fori_loop(0, cw // LANES, conv_lanes, 0)
    yb = _silu(_layernorm(cbuf[...]) * cg_ref[...] + cb_ref[...])
    ycat[:, pw:] = yb.astype(bf16)

    y = jnp.dot(ycat[...], wout_ref[...], preferred_element_type=f32) + bout_ref[...]
    x1 = _layernorm(alpha * x_ref[...] + gm_ref[...] * y) * l1g_ref[...] + l1b_ref[...]
    x1_ref[...] = x1
    h2 = (_layernorm(x1) * (1.0 + scf_ref[...]) + shf_ref[...]).astype(bf16)
    h2_ref[...] = _pack_halves(h2)
    lg_ref[...] = jnp.dot(h2, wr_ref[...], preferred_element_type=f32) + br_ref[...]


def _mix(z, x, gm, shf, scf, lw, seq, alpha, plan):
    n_tok, d = x.shape
    nz = z.shape[1]
    n_exp = lw["w_router"].shape[1]
    pw = d // 2
    cw = d - pw
    tm = plan.tm_mix
    per_seq = seq // tm
    halo_per_tile = tm // HALO
    vec = pl.BlockSpec((None, 1, d), lambda i: (i // per_seq, 0, 0))

    def whole(a):
        return pl.BlockSpec(a.shape, lambda i: (0,) * a.ndim)

    weights = [lw["w_pool"], lw["b_pool"], lw["pool_scale"], lw["w_dw"], lw["b_dw"], lw["conv_ln_g"],
               lw["conv_ln_b"], lw["w_out"], lw["b_out"], lw["ln1_g"], lw["ln1_b"], lw["w_router"], lw["b_router"]]
    kern = functools.partial(_mix_kernel, per_seq=per_seq, alpha=alpha, conv_rows=min(plan.conv_rows, tm))
    return pl.pallas_call(
        kern,
        out_shape=(jax.ShapeDtypeStruct((n_tok, d), f32),
                   jax.ShapeDtypeStruct((n_tok, d // 2), u32),
                   jax.ShapeDtypeStruct((n_tok, n_exp), f32)),
        grid=(n_tok // tm,),
        in_specs=[
            pl.BlockSpec((tm, nz), lambda i: (i, 0)),
            pl.BlockSpec((HALO, nz), lambda i: (jnp.maximum(i * halo_per_tile - 1, 0), 0)),
            pl.BlockSpec((tm, d), lambda i: (i, 0)),
            vec, vec, vec,
        ] + [whole(a) for a in weights],
        out_specs=(pl.BlockSpec((tm, d), lambda i: (i, 0)),
                   pl.BlockSpec((tm, d // 2), lambda i: (i, 0)),
                   pl.BlockSpec((tm, n_exp), lambda i: (i, 0))),
        scratch_shapes=[pltpu.VMEM((tm + HALO, pw), f32), pltpu.VMEM((tm + HALO, cw), f32),
                        pltpu.VMEM((tm, cw), f32), pltpu.VMEM((tm, d), bf16)],
        compiler_params=_params("arbitrary"),
        name="mix",
    )(z, z, x, gm, shf, scf, *weights)


def _route_kernel(lg_ref, w_ref, pos_ref, cnt_ref, idx_s, rank_s, *, tb, granule):
    n_exp, n_tok = lg_ref.shape
    iota_e = lax.broadcasted_iota(i32, (n_exp, tb), 0)
    before = (lax.broadcasted_iota(i32, (tb, tb), 0) < lax.broadcasted_iota(i32, (tb, tb), 1)).astype(bf16)

    def select(b, seen):
        blk = pl.ds(pl.multiple_of(b * tb, tb), tb)
        work = lg_ref[:, blk]
        vals, sels = [], []
        for k in range(TOP_K):
            m = jnp.max(work, axis=0, keepdims=True)
            idx = jnp.min(jnp.where(work == m, iota_e, n_exp), axis=0, keepdims=True)
            sel = iota_e == idx
            vals.append(m)
            sels.append(sel)
            idx_s[k:k + 1, blk] = idx
            work = jnp.where(sel, -jnp.inf, work)
        exps = [jnp.exp(v - vals[0]) for v in vals]
        den = exps[0]
        for e in exps[1:]:
            den = den + e
        chosen = jnp.zeros((n_exp, tb), f32)
        for k in range(TOP_K):
            w_ref[k:k + 1, blk] = exps[k] / den
            chosen = chosen + sels[k].astype(f32)
        earlier = jnp.dot(chosen.astype(bf16), before, preferred_element_type=f32) + seen
        for k in range(TOP_K):
            rank_s[k:k + 1, blk] = jnp.sum(jnp.where(sels[k], earlier, 0.0), axis=0, keepdims=True)
        return seen + jnp.sum(chosen, axis=1, keepdims=True)

    counts = lax.fori_loop(0, n_tok // tb, select, jnp.zeros((n_exp, 1), f32))

    padded = jnp.ceil(counts / granule) * granule
    ee_r = lax.broadcasted_iota(i32, (n_exp, n_exp), 0)
    ee_c = lax.broadcasted_iota(i32, (n_exp, n_exp), 1)
    padded_row = jnp.sum(jnp.where(ee_r == ee_c, padded, 0.0), axis=0, keepdims=True)
    starts = jnp.sum(jnp.where(ee_c < ee_r, padded_row, 0.0), axis=1, keepdims=True)

    def place(b, carry):
        blk = pl.ds(pl.multiple_of(b * tb, tb), tb)
        for k in range(TOP_K):
            sel = iota_e == idx_s[k:k + 1, blk]
            start = jnp.sum(jnp.where(sel, starts, 0.0), axis=0, keepdims=True)
            pos_ref[k:k + 1, blk] = (start + rank_s[k:k + 1, blk]).astype(i32)
        return carry

    lax.fori_loop(0, n_tok // tb, place, 0)
    cnt_ref[...] = jnp.broadcast_to(counts, cnt_ref.shape).astype(i32)


def _route(logits_t, plan):
    n_exp, n_tok = logits_t.shape
    kern = functools.partial(_route_kernel, tb=plan.tb_route, granule=float(plan.tm_moe))
    return pl.pallas_call(
        kern,
        out_shape=(jax.ShapeDtypeStruct((TOP_K, n_tok), f32),
                   jax.ShapeDtypeStruct((TOP_K, n_tok), i32),
                   jax.ShapeDtypeStruct((n_exp, LANES), i32)),
        scratch_shapes=[pltpu.VMEM((TOP_K, n_tok), i32), pltpu.VMEM((TOP_K, n_tok), f32)],
        compiler_params=pltpu.CompilerParams(vmem_limit_bytes=VMEM_LIMIT),
        name="route",
    )(logits_t)


def _dispatch_kernel(pstart_ref, pend_ref, nv_ref, pos_ref, h_ref, xs_ref, zrows, sem, zsem, *, tc, tm):
    c = pl.program_id(0)
    n_exp = pstart_ref.shape[0]
    zr = zrows.shape[0]

    def issue(j, carry):
        for k in range(TOP_K):
            pltpu.make_async_copy(h_ref.at[pl.ds(j, 1)], xs_ref.at[pl.ds(pos_ref[0, k * tc + j], 1)],
                                  sem).start(priority=k % 2)
        return carry
    lax.fori_loop(0, tc, issue, 0)

    step_rows = xs_ref.at[pl.ds(0, TOP_K * tc)]
    pltpu.make_async_copy(step_rows, step_rows, sem).wait()

    @pl.when(c == pl.num_programs(0) - 1)
    def _():
        zrows[...] = jnp.zeros_like(zrows)

        def row_fill(p):
            return pltpu.make_async_copy(zrows.at[pl.ds(0, 1)], xs_ref.at[pl.ds(p, 1)], zsem)

        def chunk_fill(q):
            return pltpu.make_async_copy(zrows, xs_ref.at[pl.ds(pl.multiple_of(q * zr, zr), zr)], zsem)

        def fill(lo, hi):
            mid = jnp.minimum((lo + (zr - 1)) // zr * zr, hi)
            lax.fori_loop(lo, mid, lambda p, a: (row_fill(p).start(), a)[1], 0)
            lax.fori_loop(lo, mid, lambda p, a: (row_fill(0).wait(), a)[1], 0)
            lax.fori_loop(mid // zr, hi // zr, lambda q, a: (chunk_fill(q).start(), a)[1], 0)
            lax.fori_loop(mid // zr, hi // zr, lambda q, a: (chunk_fill(0).wait(), a)[1], 0)

        def per_expert(e, carry):
            fill(pstart_ref[e], pend_ref[e])
            return carry
        lax.fori_loop(0, n_exp, per_expert, 0)
        fill(nv_ref[0] * tm, xs_ref.shape[0])


def _dispatch(h2, pos_blocks, pad_start, pad_end, n_valid, n_rows, plan):
    n_tok, dw = h2.shape
    tc = plan.tc_disp
    zr = min(plan.sub_moe, 128)
    assert plan.tm_moe % zr == 0
    kern = functools.partial(_dispatch_kernel, tc=tc, tm=plan.tm_moe)
    return pl.pallas_call(
        kern,
        out_shape=jax.ShapeDtypeStruct((n_rows, dw), h2.dtype),
        grid_spec=pltpu.PrefetchScalarGridSpec(
            num_scalar_prefetch=3,
            grid=(n_tok // tc,),
            in_specs=[
                pl.BlockSpec((None, 1, TOP_K * tc), lambda c, *_: (c, 0, 0), memory_space=pltpu.SMEM),
                pl.BlockSpec((tc, dw), lambda c, *_: (c, 0)),
            ],
            out_specs=pl.BlockSpec(memory_space=pl.ANY),
            scratch_shapes=[pltpu.VMEM((zr, dw), h2.dtype), pltpu.SemaphoreType.DMA, pltpu.SemaphoreType.DMA],
        ),
        compiler_params=_params("arbitrary"),
        name="dispatch",
    )(pad_start, pad_end, n_valid, pos_blocks, h2)


def _moe_kernel(te_ref, ns_ref, nv_ref, x_ref, wg_ref, bg_ref, wu_ref, bu_ref, wd_ref, bd_ref, o_ref,
                xb, act, wdbuf, wdsem, *, layer, n_up, n_down, tf, sub, nsub_max):
    i = pl.program_id(0)
    j = pl.program_id(1)
    valid = i < nv_ref[0]
    tile = jnp.minimum(i, nv_ref[0] - 1)
    nsub = ns_ref[tile]
    expert = te_ref[tile]
    half = x_ref.shape[1]
    td = wdbuf.shape[2]

    def wd_copy(c, slot):
        return pltpu.make_async_copy(wd_ref.at[layer, expert, :, pl.ds(pl.multiple_of(c * td, td), td)],
                                     wdbuf.at[slot], wdsem.at[slot])

    @pl.when(jnp.logical_and(valid, j == 0))
    def _():
        for c in range(min(2, n_down)):
            wd_copy(c, c).start()
        lo, hi = _unpack_halves(x_ref[...])
        xb[:, :half] = lo
        xb[:, half:] = hi

    @pl.when(jnp.logical_and(valid, j < n_up))
    def _():
        cols = pl.ds(pl.multiple_of(j * tf, tf), tf)
        for n in range(1, nsub_max + 1):
            @pl.when(nsub == n)
            def _(n=n):
                x = xb[0:n * sub, :]
                g = jnp.dot(x, wg_ref[...].astype(bf16), preferred_element_type=f32) + bg_ref[...]
                u = jnp.dot(x, wu_ref[...].astype(bf16), preferred_element_type=f32) + bu_ref[...]
                g = jnp.minimum(g, SWIGLU_LIMIT)
                u = jnp.clip(u, -SWIGLU_LIMIT, SWIGLU_LIMIT)
                a = g * jax.nn.sigmoid(SWIGLU_ALPHA * g) * (u + 1.0)
                act[0:n * sub, cols] = a.astype(bf16)

    @pl.when(jnp.logical_and(valid, j >= n_up))
    def _():
        c = j - n_up
        slot = c % 2
        wd_copy(c, slot).wait()
        for n in range(1, nsub_max + 1):
            @pl.when(nsub == n)
            def _(n=n):
                o_ref[0:n * sub, :] = jnp.dot(act[0:n * sub, :], wdbuf[slot].astype(bf16),
                                              preferred_element_type=f32) + bd_ref[...]
                if n < nsub_max:
                    o_ref[n * sub:, :] = jnp.zeros((o_ref.shape[0] - n * sub, o_ref.shape[1]), f32)

        @pl.when(c + 2 < n_down)
        def _():
            wd_copy(c + 2, slot).start()

    @pl.when(jnp.logical_and(jnp.logical_not(valid), j >= n_up))
    def _():
        o_ref[...] = jnp.zeros_like(o_ref)


def _moe(xs, tile_expert, tile_nsub, n_valid, w_gate, b_gate, w_up, b_up, w_down, b_down, layer, plan):
    n_rows, half = xs.shape
    d = 2 * half
    n_layer, n_exp, _, f = w_gate.shape
    tm, tf, td = plan.tm_moe, plan.tf_moe, plan.td_moe
    n_up, n_down = f // tf, d // td
    n_tile = n_rows // tm

    def tile(i, nv):
        return jnp.minimum(i, nv[0] - 1)

    def phase(i, j, nv):
        return jnp.where(i < nv[0], j, n_up + n_down - 1)

    def up_idx(i, j, te, ns, nv):
        return (layer, te[tile(i, nv)], 0, jnp.minimum(phase(i, j, nv), n_up - 1))

    def down_idx(i, j, te, ns, nv):
        return (layer, te[tile(i, nv)], 0, jnp.maximum(phase(i, j, nv) - n_up, 0))

    kern = functools.partial(_moe_kernel, layer=layer, n_up=n_up, n_down=n_down, tf=tf, sub=plan.sub_moe,
                             nsub_max=plan.nsub_moe)
    return pl.pallas_call(
        kern,
        out_shape=jax.ShapeDtypeStruct((n_rows, d), f32),
        grid_spec=pltpu.PrefetchScalarGridSpec(
            num_scalar_prefetch=3,
            grid=(n_tile, n_up + n_down),
            in_specs=[
                pl.BlockSpec((tm, half), lambda i, j, te, ns, nv: (tile(i, nv), 0)),
                pl.BlockSpec((None, None, d, tf), up_idx),
                pl.BlockSpec((None, None, 1, tf), up_idx),
                pl.BlockSpec((None, None, d, tf), up_idx),
                pl.BlockSpec((None, None, 1, tf), up_idx),
                pl.BlockSpec(memory_space=pl.ANY),
                pl.BlockSpec((None, None, 1, td), down_idx),
            ],
            out_specs=pl.BlockSpec((tm, td), lambda i, j, te, ns, nv: (i, jnp.maximum(j - n_up, 0))),
            scratch_shapes=[pltpu.VMEM((tm, d), bf16), pltpu.VMEM((tm, f), bf16),
                            pltpu.VMEM((2, f, td), f32), pltpu.SemaphoreType.DMA((2,))],
        ),
        compiler_params=_params("arbitrary", "arbitrary"),
        name="moe",
    )(tile_expert, tile_nsub, n_valid, xs, w_gate, b_gate.reshape(n_layer, n_exp, 1, f), w_up,
      b_up.reshape(n_layer, n_exp, 1, f), w_down, b_down.reshape(n_layer, n_exp, 1, d))


def _combine_kernel(pos_ref, nxt_ref, ys_ref, w_ref, x1_ref, gf_ref, g_ref, b_ref, o_ref, buf, sem,
                    *, tc, alpha):
    i = pl.program_id(0)
    n = pl.num_programs(0)
    slot = i % 2
    rows = TOP_K * tc

    def gather(p_ref, s):
        def issue(j, carry):
            for k in range(TOP_K):
                pltpu.make_async_copy(ys_ref.at[pl.ds(p_ref[0, k * tc + j], 1)],
                                      buf.at[s, pl.ds(k * tc + j, 1)], sem.at[s]).start(priority=k % 2)
            return carry
        lax.fori_loop(0, tc, issue, 0, unroll=ISSUE_UNROLL)

    @pl.when(i == 0)
    def _():
        gather(pos_ref, 0)

    for s in (0, 1):
        @pl.when(jnp.logical_and(i + 1 < n, slot != s))
        def _(s=s):
            gather(nxt_ref, s)

    for s in (0, 1):
        @pl.when(slot == s)
        def _(s=s):
            pltpu.make_async_copy(ys_ref.at[pl.ds(0, rows)], buf.at[s], sem.at[s]).wait()
            f = w_ref[:, 0:1] * buf[s, 0:tc, :]
            for k in range(1, TOP_K):
                f = f + w_ref[:, k:k + 1] * buf[s, k * tc:(k + 1) * tc, :]
            o_ref[...] = _layernorm(alpha * x1_ref[...] + gf_ref[...] * f) * g_ref[...] + b_ref[...]


def _combine(ys, pos_blocks, top_w, x1, gf, ln_g, ln_b, seq, alpha, plan):
    n_tok, d = x1.shape
    tc = plan.tc_comb
    n_step = n_tok // tc
    per_seq = seq // tc
    kern = functools.partial(_combine_kernel, tc=tc, alpha=alpha)
    smem = functools.partial(pl.BlockSpec, (None, 1, TOP_K * tc), memory_space=pltpu.SMEM)
    return pl.pallas_call(
        kern,
        out_shape=jax.ShapeDtypeStruct((n_tok, d), f32),
        grid=(n_step,),
        in_specs=[
            smem(lambda i: (i, 0, 0)),
            smem(lambda i: (jnp.minimum(i + 1, n_step - 1), 0, 0)),
            pl.BlockSpec(memory_space=pl.ANY),
            pl.BlockSpec((tc, TOP_K), lambda i: (i, 0)),
            pl.BlockSpec((tc, d), lambda i: (i, 0)),
            pl.BlockSpec((None, 1, d), lambda i: (i // per_seq, 0, 0)),
            pl.BlockSpec((1, d), lambda i: (0, 0)),
            pl.BlockSpec((1, d), lambda i: (0, 0)),
        ],
        out_specs=pl.BlockSpec((tc, d), lambda i: (i, 0)),
        scratch_shapes=[pltpu.VMEM((2, TOP_K * tc, d), f32), pltpu.SemaphoreType.DMA((2,))],
        compiler_params=_params("arbitrary"),
        name="combine",
    )(pos_blocks, pos_blocks, ys, top_w, x1, gf, ln_g.reshape(1, d), ln_b.reshape(1, d))


def _pos_blocks(pos, tc):
    k, n_tok = pos.shape
    return pos.reshape(k, n_tok // tc, tc).transpose(1, 0, 2).reshape(n_tok // tc, 1, k * tc)


def _tile_table(counts, tm, sub, n_tile):
    n_exp = counts.shape[0]
    tiles = (counts + tm - 1) // tm
    ends = jnp.cumsum(tiles)
    first = ends - tiles
    t = jnp.arange(n_tile, dtype=i32)
    tile_expert = jnp.minimum(jnp.sum(t[:, None] >= ends[None, :], axis=1), n_exp - 1).astype(i32)
    rows = jnp.clip(counts[tile_expert] - (t - first[tile_expert]) * tm, 0, tm)
    tile_nsub = ((rows + sub - 1) // sub).astype(i32)
    pad_start = (first * tm + counts).astype(i32)
    pad_end = (ends * tm).astype(i32)
    return tile_expert, tile_nsub, ends[-1:].astype(i32), pad_start, pad_end


def kernel(x, c, w_ada, b_ada, w_in, b_in, w_pool, b_pool, pool_scale, w_dw, b_dw, conv_ln_g, conv_ln_b,
           w_out, b_out, ln1_g, ln1_b, w_router, b_router, w_gate, b_gate, w_up, b_up, w_down, b_down,
           ln2_g, ln2_b):
    batch, seq, d = x.shape
    depth = w_ada.shape[0]
    n_exp, f = w_gate.shape[1], w_gate.shape[3]
    n_tok = batch * seq
    alpha = (2.0 * depth) ** 0.25
    plan = _plan(seq, d, n_tok, f)
    n_tile = n_tok * TOP_K // plan.tm_moe + n_exp
    n_rows = n_tile * plan.tm_moe

    mod = _ada(c, w_ada, b_ada, plan)
    xt = x.reshape(n_tok, d)
    for l in range(depth):
        sh_m, sc_m, g_m, sh_f, sc_f, g_f = [m.reshape(batch, 1, d) for m in jnp.split(mod[l], N_MOD, axis=-1)]
        row = lambda a: a[l].reshape(1, -1)
        lw = dict(
            w_pool=w_pool[l].astype(bf16), b_pool=row(b_pool), pool_scale=row(pool_scale),
            w_dw=w_dw[l], b_dw=row(b_dw), conv_ln_g=row(conv_ln_g), conv_ln_b=row(conv_ln_b),
            w_out=w_out[l].astype(bf16), b_out=row(b_out), ln1_g=row(ln1_g), ln1_b=row(ln1_b),
            w_router=w_router[l].astype(bf16), b_router=row(b_router))
        z = _inproj(xt, sh_m, sc_m, w_in[l].astype(bf16), b_in[l], seq, plan)
        x1, h2, logits = _mix(z, xt, g_m, sh_f, sc_f, lw, seq, alpha, plan)
        top_w, pos, counts = _route(logits.T, plan)
        tile_expert, tile_nsub, n_valid, pad_start, pad_end = _tile_table(
            counts[:, 0], plan.tm_moe, plan.sub_moe, n_tile)
        xs = _dispatch(h2, _pos_blocks(pos, plan.tc_disp), pad_start, pad_end, n_valid, n_rows, plan)
        ys = _moe(xs, tile_expert, tile_nsub, n_valid, w_gate, b_gate, w_up, b_up, w_down, b_down, l, plan)
        xt = _combine(ys, _pos_blocks(pos, plan.tc_comb), top_w.T, x1, g_f, ln2_g[l], ln2_b[l], seq, alpha, plan)
    return xt.reshape(batch, seq, d)
```
